```python
import math
import jax, jax.numpy as jnp
from jax import lax
import numpy as np

D_MODEL = 1024
BATCH = 4
SEQ = 8192
DEPTH = 1
DEC_BATCH = 128
DEC_SEQ = 4
PAST_LEN = 8192
PAGE_SIZE = 128

DN_HEADS = 4
DN_KDIM = 128
DN_VDIM = 128
CONV_WIDTH = 4
DN_CHUNK = 64
ATT_HEADS = 4
ATT_HDIM = 128
MOBA_BLOCK = 256
MOBA_TOPK = 3
QUERY_BLOCK = 128
NUM_BUCKETS = 32
MAX_DISTANCE = 128
DN_KW = DN_HEADS * DN_KDIM
DN_VW = DN_HEADS * DN_VDIM
ATT_W = ATT_HEADS * ATT_HDIM
MIX_WIDTH = DN_VW + ATT_W
CONV_CH = 2 * DN_KW + DN_VW
IN_COLS = CONV_CH + DN_VW + 2 * DN_HEADS + 3 * ATT_W
D_FF = 4 * D_MODEL
N_MOD = 6
EPS = 1e-6

kernel_name = 'hymba_gdn_moba_adaln_step'


def rmsnorm(x, w):
    xf = x.astype(jnp.float32)
    y = xf * lax.rsqrt(jnp.mean(xf * xf, axis=-1, keepdims=True) + EPS)
    return (y * w.astype(jnp.float32)).astype(x.dtype)


def l2norm(x):
    xf = x.astype(jnp.float32)
    return xf * lax.rsqrt(jnp.sum(xf * xf, axis=-1, keepdims=True) + EPS)


def modulation(c, w, b):
    m = jax.nn.silu(c) @ w + b
    return m.reshape(c.shape[0], N_MOD, c.shape[-1])


def causal_conv_silu(x, buf, w):
    T = x.shape[1]
    xp = jnp.concatenate([buf.astype(x.dtype), x], axis=1)
    y = xp[:, 0:T] * w[0]
    for i in range(1, CONV_WIDTH):
        y = y + xp[:, i:i + T] * w[i]
    return jax.nn.silu(y), xp[:, T:]


def gated_delta_chunked(q, k, v, g, beta, s0, chunk):
    B, T, H, DK = q.shape
    DV = v.shape[-1]
    N = T // chunk

    def blocks(a):
        a = a.reshape((B, N, chunk) + a.shape[2:])
        return jnp.moveaxis(a, 3, 2)

    qc, kc, vc, gc, bc = blocks(q), blocks(k), blocks(v), blocks(g), blocks(beta)
    G = jnp.cumsum(gc, axis=-1)
    causal = jnp.tril(jnp.ones((chunk, chunk), bool))
    strict = jnp.tril(jnp.ones((chunk, chunk), bool), -1)
    decay = jnp.exp(jnp.where(causal, G[..., :, None] - G[..., None, :], -jnp.inf))
    a_mat = jnp.where(strict, bc[..., :, None] * jnp.einsum('bnhik,bnhjk->bnhij', kc, kc) * decay, 0.0)
    lhs = a_mat + jnp.eye(chunk, dtype=a_mat.dtype)
    rhs = jnp.concatenate([vc * bc[..., None], kc * (bc * jnp.exp(G))[..., None]], axis=-1)
    sol = lax.linalg.triangular_solve(lhs, rhs, left_side=True, lower=True)
    u0, w = sol[..., :DV], sol[..., DV:]
    qk = jnp.einsum('bnhik,bnhjk->bnhij', qc, kc) * decay
    q_dec = qc * jnp.exp(G)[..., None]
    k_tail = kc * jnp.exp(G[..., -1:] - G)[..., None]
    g_last = jnp.exp(G[..., -1])

    def step(S, xs):
        u0_i, w_i, qd_i, qk_i, kt_i, gl_i = xs
        u = u0_i - jnp.einsum('bhck,bhkv->bhcv', w_i, S)
        o = jnp.einsum('bhck,bhkv->bhcv', qd_i, S) + jnp.einsum('bhij,bhjv->bhiv', qk_i, u)
        S = S * gl_i[..., None, None] + jnp.einsum('bhck,bhcv->bhkv', kt_i, u)
        return S, o

    xs = tuple(jnp.moveaxis(a, 1, 0) for a in (u0, w, q_dec, qk, k_tail, g_last))
    S, o = lax.scan(step, s0, xs)
    o = jnp.transpose(o, (1, 0, 3, 2, 4)).reshape(B, T, H, DV)
    return o, S


def delta_branch(qkv_in, z, a, b, conv_buf, s0, conv_w, a_log, dt_bias, onorm_w):
    B, T, _ = qkv_in.shape
    qkv, new_buf = causal_conv_silu(qkv_in, conv_buf, conv_w)
    q, k, v = jnp.split(qkv, [DN_KW, 2 * DN_KW], axis=-1)
    q = l2norm(q.reshape(B, T, DN_HEADS, DN_KDIM)) * (DN_KDIM ** -0.5)
    k = l2norm(k.reshape(B, T, DN_HEADS, DN_KDIM))
    v = v.reshape(B, T, DN_HEADS, DN_VDIM).astype(jnp.float32)
    g = -jnp.exp(a_log.astype(jnp.float32)) * jax.nn.softplus(a.astype(jnp.float32) + dt_bias.astype(jnp.float32))
    beta = jax.nn.sigmoid(b.astype(jnp.float32))
    chunk = DN_CHUNK if T % DN_CHUNK == 0 else T
    o, s_new = gated_delta_chunked(q, k, v, g, beta, s0.astype(jnp.float32), chunk)
    zg = jax.nn.silu(z.reshape(B, T, DN_HEADS, DN_VDIM).astype(jnp.float32))
    o = rmsnorm(o, onorm_w) * zg
    return o.reshape(B, T, DN_VW).astype(qkv_in.dtype), new_buf, s_new


def t5_bucket(dist):
    n = jnp.maximum(dist, 0)
    max_exact = NUM_BUCKETS // 2
    nf = jnp.maximum(n, 1).astype(jnp.float32)
    large = max_exact + (jnp.log(nf / max_exact) / math.log(MAX_DISTANCE / max_exact)
                         * (NUM_BUCKETS - max_exact)).astype(jnp.int32)
    large = jnp.minimum(large, NUM_BUCKETS - 1)
    return jnp.where(n < max_exact, n, large)


def to_blocks(k):
    L, H, D = k.shape
    nb = -(-L // MOBA_BLOCK)
    k = jnp.pad(k, ((0, nb * MOBA_BLOCK - L), (0, 0), (0, 0)))
    return k.reshape(nb, MOBA_BLOCK, H, D).transpose(2, 0, 1, 3)


def moba_core(q, q_pos, kb, vb, kmean, rel_bias):
    Q, H, D = q.shape
    nb = kb.shape[1]
    n_sel = min(MOBA_TOPK, nb)
    own = q_pos // MOBA_BLOCK
    gate = jnp.einsum('qhd,hnd->qhn', q.astype(jnp.float32), kmean)
    past_blk = jnp.arange(nb)[None, None, :] < own[:, None, None]
    gate = jnp.where(past_blk, gate, -jnp.inf)
    _, sel = lax.top_k(gate, n_sel)
    blocks = jnp.concatenate([sel.astype(jnp.int32),
                              jnp.broadcast_to(own[:, None, None], (Q, H, 1)).astype(jnp.int32)], axis=-1)
    slot_ok = jnp.concatenate([jnp.broadcast_to(jnp.arange(n_sel)[None, None, :] < own[:, None, None], (Q, H, n_sel)),
                               jnp.ones((Q, H, 1), bool)], axis=-1)
    h_idx = jnp.arange(H)[None, :, None]
    k_sel = kb[h_idx, blocks]
    v_sel = vb[h_idx, blocks]
    key_pos = blocks[..., None] * MOBA_BLOCK + jnp.arange(MOBA_BLOCK, dtype=jnp.int32)
    dist = q_pos[:, None, None, None] - key_pos
    logits = jnp.einsum('qhd,qhsjd->qhsj', q, k_sel).astype(jnp.float32) * (ATT_HDIM ** -0.5)
    logits = logits + rel_bias.astype(jnp.float32)[h_idx[..., None], t5_bucket(dist)]
    logits = jnp.where(slot_ok[..., None] & (dist >= 0), logits, -jnp.inf)
    S = blocks.shape[-1]
    p = jax.nn.softmax(logits.reshape(Q, H, S * MOBA_BLOCK), axis=-1).reshape(Q, H, S, MOBA_BLOCK)
    return jnp.einsum('qhsj,qhsjd->qhd', p.astype(v_sel.dtype), v_sel)


def moba_prompt(q, k, v, rel_bias):
    B, T, H, D = q.shape
    kb = jax.vmap(to_blocks)(k)
    vb = jax.vmap(to_blocks)(v)
    kmean = jnp.mean(kb.astype(jnp.float32), axis=-2)
    nq = T // QUERY_BLOCK

    def one(i):
        b = i // nq
        start = (i % nq) * QUERY_BLOCK
        q_blk = lax.dynamic_slice(q, (b, start, 0, 0), (1, QUERY_BLOCK, H, D))[0]
        pos = start + jnp.arange(QUERY_BLOCK, dtype=jnp.int32)
        return moba_core(q_blk, pos, kb[b], vb[b], kmean[b], rel_bias)

    out = lax.map(one, jnp.arange(B * nq, dtype=jnp.int32))
    return out.reshape(B, T, H * D)


def moba_decode(q, k, v, cache_k, cache_v, layer, page_table, rel_bias):
    DB, T, H, D = q.shape
    past = page_table.shape[1] * cache_k.shape[2]
    pos = past + jnp.arange(T, dtype=jnp.int32)

    def one(args):
        pt, q_s, k_s, v_s = args
        k_full = jnp.concatenate([cache_k[layer, pt].reshape(past, H, D), k_s.astype(cache_k.dtype)], axis=0)
        v_full = jnp.concatenate([cache_v[layer, pt].reshape(past, H, D), v_s.astype(cache_v.dtype)], axis=0)
        kb = to_blocks(k_full)
        vb = to_blocks(v_full)
        kmean = jnp.mean(kb.astype(jnp.float32), axis=-2)
        return moba_core(q_s, pos, kb, vb, kmean, rel_bias).astype(q.dtype)

    out = lax.map(one, (page_table, q, k, v))
    return out.reshape(DB, T, H * D)


def mixer(h, conv_buf, s0, attend, w_in, w_out, conv_w, a_log, dt_bias, onorm_w):
    B, T, _ = h.shape
    widths = [CONV_CH, DN_VW, DN_HEADS, DN_HEADS, ATT_W, ATT_W, ATT_W]
    offsets = np.cumsum(widths)[:-1].tolist()
    proj = h @ w_in
    qkv_d, z, a, b, q_a, k_a, v_a = jnp.split(proj, offsets, axis=-1)
    o_d, new_buf, s_new = delta_branch(qkv_d, z, a, b, conv_buf, s0, conv_w, a_log, dt_bias, onorm_w)
    q_a = q_a.reshape(B, T, ATT_HEADS, ATT_HDIM)
    k_a = k_a.reshape(B, T, ATT_HEADS, ATT_HDIM)
    v_a = v_a.reshape(B, T, ATT_HEADS, ATT_HDIM)
    o_a = attend(q_a, k_a, v_a).astype(h.dtype)
    out = jnp.concatenate([o_d, o_a], axis=-1) @ w_out
    return out, new_buf, s_new, k_a, v_a


def block(x, mod, conv_buf, s0, attend, n1, n2, w_in, w_out, conv_w, a_log, dt_bias, onorm_w, w1, w2):
    h = rmsnorm(x, n1) * (1 + mod[:, 1, None]) + mod[:, 0, None]
    mix, new_buf, s_new, k_a, v_a = mixer(h, conv_buf, s0, attend, w_in, w_out, conv_w, a_log, dt_bias, onorm_w)
    x = x + mod[:, 2, None] * mix
    h = rmsnorm(x, n2) * (1 + mod[:, 4, None]) + mod[:, 3, None]
    x = x + mod[:, 5, None] * (jnp.square(jax.nn.relu(h @ w1)) @ w2)
    return x, new_buf, s_new, k_a, v_a


def setup_inputs(seed: int = 0) -> dict:
    key = jax.random.key(seed)
    ks = jax.random.split(key, 24)
    nrm = jax.random.normal
    f32 = jnp.float32
    n_pages = PAST_LEN // PAGE_SIZE
    n_used = DEC_BATCH * n_pages
    n_pool = n_used + max(1, n_used // 4)
    x_prompt = nrm(ks[0], (BATCH, SEQ, D_MODEL), f32)
    x_sample = nrm(ks[1], (DEC_BATCH, DEC_SEQ, D_MODEL), f32)
    cache_k = nrm(ks[2], (DEPTH, n_pool, PAGE_SIZE, ATT_HEADS, ATT_HDIM), f32)
    cache_v = nrm(ks[3], (DEPTH, n_pool, PAGE_SIZE, ATT_HEADS, ATT_HDIM), f32)
    state_delta = 0.1 * nrm(ks[4], (DEPTH, DEC_BATCH, DN_HEADS, DN_KDIM, DN_VDIM), f32)
    state_conv = nrm(ks[5], (DEPTH, DEC_BATCH, CONV_WIDTH - 1, CONV_CH), f32)
    page_table = jax.random.permutation(ks[6], n_pool)[:n_used].reshape(DEC_BATCH, n_pages).astype(jnp.int32)
    c_prompt = nrm(ks[7], (BATCH, D_MODEL), f32)
    c_sample = nrm(ks[8], (DEC_BATCH, D_MODEL), f32)
    norm1_w = 1.0 + 0.01 * nrm(ks[9], (DEPTH, D_MODEL), f32)
    norm2_w = 1.0 + 0.01 * nrm(ks[10], (DEPTH, D_MODEL), f32)
    normf_w = 1.0 + 0.01 * nrm(ks[11], (D_MODEL,), f32)
    w_ada = 0.5 * D_MODEL ** -0.5 * nrm(ks[12], (DEPTH, D_MODEL, N_MOD * D_MODEL), f32)
    b_ada = 0.01 * nrm(ks[13], (DEPTH, N_MOD * D_MODEL), f32)
    w_in = D_MODEL ** -0.5 * nrm(ks[14], (DEPTH, D_MODEL, IN_COLS), f32)
    w_out = MIX_WIDTH ** -0.5 * nrm(ks[15], (DEPTH, MIX_WIDTH, D_MODEL), f32)
    conv_w = CONV_WIDTH ** -0.5 * nrm(ks[16], (DEPTH, CONV_WIDTH, CONV_CH), f32)
    a_log = jnp.log(jax.random.uniform(ks[17], (DEPTH, DN_HEADS), f32, 1.0, 16.0))
    dt = jnp.exp(jax.random.uniform(ks[18], (DEPTH, DN_HEADS), f32, math.log(1e-3), math.log(1e-1)))
    dt_bias = dt + jnp.log(-jnp.expm1(-dt))
    onorm_w = 1.0 + 0.01 * nrm(ks[19], (DEPTH, DN_VDIM), f32)
    rel_bias = 0.5 * nrm(ks[20], (ATT_HEADS, NUM_BUCKETS), f32)
    w_mlp1 = D_MODEL ** -0.5 * nrm(ks[21], (DEPTH, D_MODEL, D_FF), f32)
    w_mlp2 = D_FF ** -0.5 * nrm(ks[22], (DEPTH, D_FF, D_MODEL), f32)
    return {'x_prompt': x_prompt, 'x_sample': x_sample, 'cache_k': cache_k, 'cache_v': cache_v,
            'state_delta': state_delta, 'state_conv': state_conv, 'page_table': page_table,
            'c_prompt': c_prompt, 'c_sample': c_sample, 'norm1_w': norm1_w, 'norm2_w': norm2_w,
            'normf_w': normf_w, 'w_ada': w_ada, 'b_ada': b_ada, 'w_in': w_in, 'w_out': w_out,
            'conv_w': conv_w, 'a_log': a_log, 'dt_bias': dt_bias, 'onorm_w': onorm_w,
            'rel_bias': rel_bias, 'w_mlp1': w_mlp1, 'w_mlp2': w_mlp2}


def reference(x_prompt, x_sample, cache_k, cache_v, state_delta, state_conv, page_table,
              c_prompt, c_sample, norm1_w, norm2_w, normf_w, w_ada, b_ada, w_in, w_out,
              conv_w, a_log, dt_bias, onorm_w, rel_bias, w_mlp1, w_mlp2):
    xp, xs = x_prompt, x_sample
    Bp = xp.shape[0]
    kp, vp, sp, cp, ksl, vsl, ssl, csl = [], [], [], [], [], [], [], []
    for l in range(DEPTH):
        lw = (norm1_w[l], norm2_w[l], w_in[l], w_out[l], conv_w[l], a_log[l], dt_bias[l],
              onorm_w[l], w_mlp1[l], w_mlp2[l])
        mod_p = modulation(c_prompt, w_ada[l], b_ada[l])
        mod_s = modulation(c_sample, w_ada[l], b_ada[l])
        buf0 = jnp.zeros((Bp, CONV_WIDTH - 1, CONV_CH), xp.dtype)
        s00 = jnp.zeros((Bp, DN_HEADS, DN_KDIM, DN_VDIM), jnp.float32)
        attend_p = lambda q, k, v: moba_prompt(q, k, v, rel_bias)
        xp, b_p, s_p, k_p, v_p = block(xp, mod_p, buf0, s00, attend_p, *lw)
        attend_s = lambda q, k, v, l=l: moba_decode(q, k, v, cache_k, cache_v, l, page_table, rel_bias)
        xs, b_s, s_s, k_s, v_s = block(xs, mod_s, state_conv[l], state_delta[l], attend_s, *lw)
        kp.append(k_p.astype(cache_k.dtype)); vp.append(v_p.astype(cache_v.dtype))
        sp.append(s_p.astype(state_delta.dtype)); cp.append(b_p.astype(state_conv.dtype))
        ksl.append(k_s.astype(cache_k.dtype)); vsl.append(v_s.astype(cache_v.dtype))
        ssl.append(s_s.astype(state_delta.dtype)); csl.append(b_s.astype(state_conv.dtype))
    y_prompt = rmsnorm(xp, normf_w)
    y_sample = rmsnorm(xs, normf_w)
    return (y_prompt, y_sample, jnp.stack(kp), jnp.stack(vp), jnp.stack(sp), jnp.stack(cp),
            jnp.stack(ksl), jnp.stack(vsl), jnp.stack(ssl), jnp.stack(csl))
```

```python
import functools
import math

import numpy as np
import jax
import jax.numpy as jnp
from jax import lax
from jax.experimental import pallas as pl
from jax.experimental.pallas import tpu as pltpu

F32 = jnp.float32
BF16 = jnp.bfloat16
HI = lax.Precision.HIGHEST
NEG_INF = float("-inf")

EPS = 1e-6
N_HEADS = 4
HEAD_DIM = 128
GROUP_W = N_HEADS * HEAD_DIM
CONV_W = 4
CONV_CH = 3 * GROUP_W
CHUNK = 64
STACK = N_HEADS * CHUNK
MOBA_BLOCK = 256
MOBA_TOPK = 3
NUM_BUCKETS = 32
MAX_DISTANCE = 128
LANES = 128
VMEM_LIMIT = 56 * 1024 * 1024


def _bucket_np(dist):
    n = np.maximum(dist, 0)
    max_exact = NUM_BUCKETS // 2
    nf = np.maximum(n, 1).astype(np.float32)
    large = max_exact + (np.log(nf / max_exact) / math.log(MAX_DISTANCE / max_exact)
                         * (NUM_BUCKETS - max_exact)).astype(np.int32)
    large = np.minimum(large, NUM_BUCKETS - 1)
    return np.where(n < max_exact, n, large).astype(np.int32)


def _mm(a, b):
    return jnp.dot(a.astype(BF16), b.astype(BF16), preferred_element_type=F32)


def _mm_nt(a, b):
    return lax.dot_general(a.astype(BF16), b.astype(BF16), (((1,), (1,)), ((), ())),
                           preferred_element_type=F32)


def _mm_tn(a, b):
    return lax.dot_general(a.astype(BF16), b.astype(BF16), (((0,), (0,)), ((), ())),
                           preferred_element_type=F32)


def _silu(x):
    return x * jax.nn.sigmoid(x)


def _softplus(x):
    return jnp.maximum(x, 0.0) + jnp.log1p(jnp.exp(-jnp.abs(x)))


def _mod_kernel(c_ref, w_ref, b_ref, o_ref):
    o_ref[...] = jnp.dot(_silu(c_ref[...]), w_ref[...], precision=HI,
                         preferred_element_type=F32) + b_ref[...]


def _modulation(c, w, b, tn=1536):
    m, d = c.shape
    n = w.shape[1]
    return pl.pallas_call(
        _mod_kernel,
        grid=(n // tn,),
        in_specs=[pl.BlockSpec((m, d), lambda j: (0, 0)),
                  pl.BlockSpec((d, tn), lambda j: (0, j)),
                  pl.BlockSpec((1, tn), lambda j: (0, j))],
        out_specs=pl.BlockSpec((m, tn), lambda j: (0, j)),
        out_shape=jax.ShapeDtypeStruct((m, n), F32),
        compiler_params=pltpu.CompilerParams(dimension_semantics=("arbitrary",),
                                             vmem_limit_bytes=VMEM_LIMIT),
        name="modulation",
    )(c, w, b.reshape(1, n))


def _inproj_kernel(x_ref, sh_ref, sc_ref, n1_ref, wm_ref, wab_ref, alog_ref, dtb_ref,
                   qkv_ref, z_ref, qa_ref, ka_ref, va_ref, gb_ref, gbt_ref):
    x = x_ref[...]
    r = lax.rsqrt(jnp.mean(x * x, axis=-1, keepdims=True) + EPS)
    h = (x * r * n1_ref[...]) * (1.0 + sc_ref[...]) + sh_ref[...]
    hb = h.astype(BF16)
    off = 0
    for ref in (qkv_ref, z_ref, qa_ref, ka_ref, va_ref):
        w = ref.shape[-1]
        ref[...] = jnp.dot(hb, wm_ref[:, off:off + w], preferred_element_type=F32)
        off += w
    ab = jnp.dot(h, wab_ref[...], precision=HI, preferred_element_type=F32)
    lane = lax.broadcasted_iota(jnp.int32, ab.shape, 1)
    g = -jnp.exp(alog_ref[...]) * _softplus(ab + dtb_ref[...])
    gb = jnp.where(lane < N_HEADS, g, jax.nn.sigmoid(ab))
    gb_ref[...] = gb
    gbt_ref[...] = gb.T[0:8, :]


def _inproj(x, shift, scale, n1, w_main, w_ab, alog, dtb, tm, rows_per_mod):
    t, d = x.shape
    nt = t // tm
    if shift.shape[1] == 1:
        per = rows_per_mod // tm
        mod_spec = pl.BlockSpec((None, 1, d), lambda i: (i // per, 0, 0))
    else:
        mod_spec = pl.BlockSpec((None, tm, d), lambda i: (i, 0, 0))
    const = lambda i: (0, 0)
    widths = (CONV_CH, GROUP_W, GROUP_W, GROUP_W, GROUP_W)
    outs = [jax.ShapeDtypeStruct((t, w), F32) for w in widths]
    outs += [jax.ShapeDtypeStruct((t, LANES), F32), jax.ShapeDtypeStruct((8, t), F32)]
    out_specs = [pl.BlockSpec((tm, w), lambda i: (i, 0)) for w in widths]
    out_specs += [pl.BlockSpec((tm, LANES), lambda i: (i, 0)), pl.BlockSpec((8, tm), lambda i: (0, i))]
    return pl.pallas_call(
        _inproj_kernel,
        grid=(nt,),
        in_specs=[pl.BlockSpec((tm, d), lambda i: (i, 0)), mod_spec, mod_spec,
                  pl.BlockSpec((1, d), const),
                  pl.BlockSpec(w_main.shape, const),
                  pl.BlockSpec(w_ab.shape, const),
                  pl.BlockSpec((1, LANES), const), pl.BlockSpec((1, LANES), const)],
        out_specs=out_specs,
        out_shape=outs,
        compiler_params=pltpu.CompilerParams(dimension_semantics=("arbitrary",),
                                             vmem_limit_bytes=VMEM_LIMIT),
        name="inproj",
    )(x, shift, scale, n1, w_main, w_ab, alog, dtb)


def _gdn_kernel(x_ref, gb_ref, gbt_ref, z_ref, cs_ref, s0_ref, cw_ref, onw_ref,
                o_ref, sn_ref, xbuf, s_scr, *, t_real, n_chunks):
    ti = pl.program_id(1)
    t_tile = n_chunks * CHUNK

    @pl.when(ti == 0)
    def _():
        xbuf[0:8, :] = jnp.zeros((8, CONV_CH), F32)
        xbuf[5:8, :] = cs_ref[...]
        s_scr[...] = s0_ref[...]
        if t_real < t_tile:
            xbuf[8:8 + t_tile, :] = jnp.zeros((t_tile, CONV_CH), F32)

    xbuf[8:8 + t_real, :] = x_ref[...]

    ri = lax.broadcasted_iota(jnp.int32, (STACK, STACK), 0)
    ci = lax.broadcasted_iota(jnp.int32, (STACK, STACK), 1)
    same = (ri >> 6) == (ci >> 6)
    causal = same & ((ri & 63) >= (ci & 63))
    strict = same & ((ri & 63) > (ci & 63))
    eye = (ri == ci).astype(F32)
    l_st = (lax.broadcasted_iota(jnp.int32, (STACK, CHUNK), 1)
            <= (lax.broadcasted_iota(jnp.int32, (STACK, CHUNK), 0) & 63)).astype(F32)
    u_st = (lax.broadcasted_iota(jnp.int32, (CHUNK, STACK), 0)
            <= (lax.broadcasted_iota(jnp.int32, (CHUNK, STACK), 1) & 63)).astype(F32)
    row_head = lax.broadcasted_iota(jnp.int32, (STACK, LANES), 0) >> 6
    lane_id = lax.broadcasted_iota(jnp.int32, (STACK, LANES), 1)
    pick_g = lane_id == row_head
    pick_b = lane_id == row_head + N_HEADS
    col_head8 = lax.broadcasted_iota(jnp.int32, (8, STACK), 1) >> 6
    sub8 = lax.broadcasted_iota(jnp.int32, (8, STACK), 0)
    cw = cw_ref[...]
    onw = onw_ref[...]

    def stack(a):
        return jnp.concatenate([a[:, h * HEAD_DIM:(h + 1) * HEAD_DIM] for h in range(N_HEADS)], axis=0)

    def l2n(a):
        return a * lax.rsqrt(jnp.sum(a * a, axis=-1, keepdims=True) + EPS)

    for c in range(n_chunks):
        base = 8 + c * CHUNK
        y = xbuf[base - 3:base - 3 + CHUNK, :] * cw[0:1, :]
        for i in range(1, CONV_W):
            y = y + xbuf[base - 3 + i:base - 3 + i + CHUNK, :] * cw[i:i + 1, :]
        y = _silu(y)
        q = l2n(stack(y[:, 0:GROUP_W])) * (HEAD_DIM ** -0.5)
        k = l2n(stack(y[:, GROUP_W:2 * GROUP_W]))
        v = stack(y[:, 2 * GROUP_W:3 * GROUP_W])

        gb = gb_ref[c * CHUNK:(c + 1) * CHUNK, :]
        gbt = gbt_ref[:, c * CHUNK:(c + 1) * CHUNK]
        gcum = jnp.dot(l_st, gb, precision=HI, preferred_element_type=F32)
        g_col = jnp.sum(jnp.where(pick_g, gcum, 0.0), axis=1, keepdims=True)
        g_tot = jnp.sum(jnp.where(pick_g, jnp.broadcast_to(gcum[CHUNK - 1:CHUNK, :], (STACK, LANES)), 0.0),
                        axis=1, keepdims=True)
        gb_st = jnp.concatenate([gb] * N_HEADS, axis=0)
        beta = jnp.sum(jnp.where(pick_b, gb_st, 0.0), axis=1, keepdims=True)
        grow_all = jnp.dot(gbt, u_st, precision=HI, preferred_element_type=F32)
        g_row = jnp.sum(jnp.where(sub8 == col_head8, grow_all, 0.0), axis=0, keepdims=True)

        decay = jnp.exp(jnp.where(causal, g_col - g_row, NEG_INF))
        exp_g = jnp.exp(g_col)
        kb = k.astype(BF16)
        a_mat = jnp.where(strict, beta * _mm_nt(kb, kb) * decay, 0.0)
        inv = eye - a_mat
        pw = a_mat
        for _ in range(5):
            pw = _mm(pw, pw)
            inv = inv + _mm(inv, pw)
        resid = eye - jnp.dot(eye + a_mat, inv, precision=HI, preferred_element_type=F32)
        inv = inv + _mm(inv, resid)
        rhs =jnp.concatenate([v * beta, k * (beta * exp_g)], axis=1)
        sol = _mm(inv, rhs)
        u0 = sol[:, 0:HEAD_DIM]
        w = sol[:, HEAD_DIM:2 * HEAD_DIM]
        qk = _mm_nt(q, kb) * decay
        q_dec = q * exp_g
        k_tail = k * jnp.exp(g_tot - g_col)

        us, qs = [], []
        for h in range(N_HEADS):
            sl = slice(h * CHUNK, (h + 1) * CHUNK)
            wq = jnp.concatenate([w[sl], q_dec[sl]], axis=0)
            r = _mm(wq, s_scr[h])
            us.append(u0[sl] - r[0:CHUNK])
            qs.append(r[CHUNK:2 * CHUNK])
        u = jnp.concatenate(us, axis=0)
        o = jnp.concatenate(qs, axis=0) + _mm(qk, u)
        for h in range(N_HEADS):
            sl = slice(h * CHUNK, (h + 1) * CHUNK)
            gl = jnp.exp(gcum[CHUNK - 1:CHUNK, h:h + 1])
            s_scr[h] = s_scr[h] * gl + _mm_tn(k_tail[sl], u[sl])

        o = o * lax.rsqrt(jnp.mean(o * o, axis=-1, keepdims=True) + EPS) * onw
        rows = min(CHUNK, t_real - c * CHUNK)
        for h in range(N_HEADS):
            zg = _silu(z_ref[c * CHUNK:c * CHUNK + rows, h * HEAD_DIM:(h + 1) * HEAD_DIM])
            o_ref[c * CHUNK:c * CHUNK + rows, h * HEAD_DIM:(h + 1) * HEAD_DIM] = \
                o[h * CHUNK:h * CHUNK + rows] * zg

    if t_real == t_tile:
        xbuf[5:8, :] = xbuf[8 + t_tile - 3:8 + t_tile, :]

    @pl.when(ti == pl.num_programs(1) - 1)
    def _():
        sn_ref[...] = s_scr[...]


def _gdn(qkv, gb, gbt, z, conv_state, s0, conv_w, onorm_w, n_chunks):
    b, t, _ = qkv.shape
    t_tile = n_chunks * CHUNK
    t_real = min(t, t_tile)
    nt = max(1, t // t_tile)
    kern = functools.partial(_gdn_kernel, t_real=t_real, n_chunks=n_chunks)
    return pl.pallas_call(
        kern,
        grid=(b, nt),
        in_specs=[pl.BlockSpec((None, t_real, CONV_CH), lambda i, j: (i, j, 0)),
                  pl.BlockSpec((None, t_tile, LANES), lambda i, j: (i, j, 0)),
                  pl.BlockSpec((None, 8, t_tile), lambda i, j: (i, 0, j)),
                  pl.BlockSpec((None, t_real, GROUP_W), lambda i, j: (i, j, 0)),
                  pl.BlockSpec((None, CONV_W - 1, CONV_CH), lambda i, j: (i, 0, 0)),
                  pl.BlockSpec((None, N_HEADS, HEAD_DIM, HEAD_DIM), lambda i, j: (i, 0, 0, 0)),
                  pl.BlockSpec((CONV_W, CONV_CH), lambda i, j: (0, 0)),
                  pl.BlockSpec((1, HEAD_DIM), lambda i, j: (0, 0))],
        out_specs=[pl.BlockSpec((None, t_real, GROUP_W), lambda i, j: (i, j, 0)),
                   pl.BlockSpec((None, N_HEADS, HEAD_DIM, HEAD_DIM), lambda i, j: (i, 0, 0, 0))],
        out_shape=[jax.ShapeDtypeStruct((b, t, GROUP_W), F32),
                   jax.ShapeDtypeStruct((b, N_HEADS, HEAD_DIM, HEAD_DIM), F32)],
        scratch_shapes=[pltpu.VMEM((8 + t_tile, CONV_CH), F32),
                        pltpu.VMEM((N_HEADS, HEAD_DIM, HEAD_DIM), F32)],
        compiler_params=pltpu.CompilerParams(dimension_semantics=("arbitrary", "arbitrary"),
                                             vmem_limit_bytes=VMEM_LIMIT),
        name="gated_delta",
    )(qkv, gb, gbt, z, conv_state, s0, conv_w, onorm_w)


def _bias_from_buckets(bkt, relb_ref, h):
    def body(t, b):
        return jnp.where(bkt == t, relb_ref[h, t], b)
    return lax.fori_loop(0, NUM_BUCKETS, body, jnp.zeros(bkt.shape, F32))


def _moba_prompt_kernel(relb_ref, bkt_ref, q_ref, k_ref, v_ref, o_ref,
                        kb_scr, vb_scr, kmean_scr, bias_scr, *, n_blocks):
    h = pl.program_id(1)
    qi = pl.program_id(2)
    blk = MOBA_BLOCK
    scale = HEAD_DIM ** -0.5

    @pl.when(qi == 0)
    def _():
        kb_scr[...] = k_ref[...].astype(BF16)
        vb_scr[...] = v_ref[...].astype(BF16)
        for j in range(n_blocks):
            kmean_scr[j:j + 1, :] = jnp.mean(k_ref[j * blk:(j + 1) * blk, :], axis=0, keepdims=True)
        for s in range(2):
            bias_scr[s] = _bias_from_buckets(bkt_ref[s], relb_ref, h)

    q = q_ref[...]
    qb = q.astype(BF16)
    far_bias = relb_ref[h, NUM_BUCKETS - 1]

    gate = lax.dot_general(q, kmean_scr[...], (((1,), (1,)), ((), ())), precision=HI,
                           preferred_element_type=F32)
    lane = lax.broadcasted_iota(jnp.int32, gate.shape, 1)
    g = jnp.where(lane < qi, gate, NEG_INF)
    sel = jnp.zeros(gate.shape, F32)
    for s in range(MOBA_TOPK):
        mx = jnp.max(g, axis=1, keepdims=True)
        idx = jnp.min(jnp.where(g == mx, lane, n_blocks), axis=1, keepdims=True)
        hit = lane == idx
        slot_ok = jnp.where(qi > s, 1.0, 0.0)
        sel = jnp.maximum(sel, jnp.where(hit, slot_ok, 0.0))
        g = jnp.where(hit, NEG_INF, g)

    ri = lax.broadcasted_iota(jnp.int32, (blk, blk), 0)
    ci = lax.broadcasted_iota(jnp.int32, (blk, blk), 1)
    own0 = pl.multiple_of(qi * blk, blk)
    s0 = _mm_nt(qb, kb_scr[pl.ds(own0, blk), :]) * scale + bias_scr[0]
    s0 = jnp.where(ri >= ci, s0, NEG_INF)
    m0 = jnp.max(s0, axis=1, keepdims=True)
    p0 = jnp.exp(s0 - m0)
    l0 = jnp.sum(p0, axis=1, keepdims=True)
    acc0 = _mm(p0, vb_scr[pl.ds(own0, blk), :])

    def body(j, carry):
        m, l, acc = carry
        start = pl.multiple_of(j * blk, blk)
        chosen = jnp.sum(jnp.where(lane == j, sel, 0.0), axis=1, keepdims=True) > 0.0
        bias = jnp.where(j == qi - 1, bias_scr[1], far_bias)
        s = _mm_nt(qb, kb_scr[pl.ds(start, blk), :]) * scale + bias
        s = jnp.where(chosen, s, NEG_INF)
        m_new = jnp.maximum(m, jnp.max(s, axis=1, keepdims=True))
        alpha = jnp.exp(m - m_new)
        p = jnp.exp(s - m_new)
        l = alpha * l + jnp.sum(p, axis=1, keepdims=True)
        acc = alpha * acc + _mm(p, vb_scr[pl.ds(start, blk), :])
        return m_new, l, acc

    m, l, acc = lax.fori_loop(0, qi, body, (m0, l0, acc0))
    o_ref[...] = acc / l


def _moba_prompt(q, k, v, rel_bias):
    b, t, _ = q.shape
    blk = MOBA_BLOCK
    nb = t // blk
    r = np.arange(blk)[:, None]
    c = np.arange(blk)[None, :]
    bkt = jnp.asarray(np.stack([_bucket_np(r - c), _bucket_np(blk + r - c)]))
    kern = functools.partial(_moba_prompt_kernel, n_blocks=nb)
    return pl.pallas_call(
        kern,
        grid=(b, N_HEADS, nb),
        in_specs=[pl.BlockSpec(memory_space=pltpu.SMEM),
                  pl.BlockSpec((2, blk, blk), lambda i, h, j: (0, 0, 0)),
                  pl.BlockSpec((None, blk, HEAD_DIM), lambda i, h, j: (i, j, h)),
                  pl.BlockSpec((None, t, HEAD_DIM), lambda i, h, j: (i, 0, h)),
                  pl.BlockSpec((None, t, HEAD_DIM), lambda i, h, j: (i, 0, h))],
        out_specs=pl.BlockSpec((None, blk, HEAD_DIM), lambda i, h, j: (i, j, h)),
        out_shape=jax.ShapeDtypeStruct((b, t, GROUP_W), F32),
        scratch_shapes=[pltpu.VMEM((t, HEAD_DIM), BF16), pltpu.VMEM((t, HEAD_DIM), BF16),
                        pltpu.VMEM((nb, HEAD_DIM), F32), pltpu.VMEM((2, blk, blk), F32)],
        compiler_params=pltpu.CompilerParams(dimension_semantics=("arbitrary", "arbitrary", "arbitrary"),
                                             vmem_limit_bytes=VMEM_LIMIT),
        name="moba_prompt",
    )(rel_bias, bkt, q, k, v)


def _moba_decode_kernel(pt_ref, relb_ref, bkt_ref, q_ref, kn_ref, vn_ref, *rest,
                        pages_per_step, n_blocks, n_new):
    del pt_ref
    npg = pages_per_step
    k_refs = rest[0:npg]
    v_refs = rest[npg:2 * npg]
    o_ref = rest[2 * npg]
    qbd_scr, m_scr, l_scr, acc_scr, kmean_scr, bias_scr = rest[2 * npg + 1:]
    jb = pl.program_id(1)
    blk = MOBA_BLOCK
    n_pairs = n_new * N_HEADS
    scale = HEAD_DIM ** -0.5
    bps = npg // 2

    lane1 = lax.broadcasted_iota(jnp.int32, (1, LANES), 1)
    far_row = jnp.zeros((1, LANES), F32)
    for h in range(N_HEADS):
        far_row = jnp.where((lane1 & 3) == h, relb_ref[h, NUM_BUCKETS - 1], far_row)

    @pl.when(jb == 0)
    def _():
        q = q_ref[...]
        rows = jnp.concatenate([jnp.broadcast_to(q[t:t + 1, :], (N_HEADS, GROUP_W)) for t in range(n_new)], axis=0)
        rp = lax.broadcasted_iota(jnp.int32, (n_pairs, GROUP_W), 0)
        cp = lax.broadcasted_iota(jnp.int32, (n_pairs, GROUP_W), 1)
        qbd = jnp.where((rp & 3) == (cp >> 7), rows, 0.0)
        qbd_scr[...] = jnp.zeros(qbd_scr.shape, F32)
        qbd_scr[0:n_pairs, :] = qbd
        bkt = bkt_ref[...]
        lane_h = lax.broadcasted_iota(jnp.int32, bkt.shape, 1) & 3
        b = jnp.zeros(bkt.shape, F32)
        for h in range(N_HEADS):
            b = jnp.where(lane_h == h, _bias_from_buckets(bkt, relb_ref, h), b)
        bias_scr[...] = b

    qbd = qbd_scr[...]
    for i in range(bps):
        j = jb * bps + i
        kblk = jnp.concatenate([k_refs[2 * i][...], k_refs[2 * i + 1][...]], axis=0)
        vblk = jnp.concatenate([v_refs[2 * i][...], v_refs[2 * i + 1][...]], axis=0)
        kmean_scr[pl.ds(j, 1), :] = jnp.mean(kblk, axis=0, keepdims=True)
        bias = jnp.where(j == n_blocks - 1, bias_scr[...], far_row)
        s = _mm_nt(kblk, qbd) * scale + bias
        mj = jnp.max(s, axis=0, keepdims=True)
        p = jnp.exp(s - mj)
        m_scr[pl.ds(j, 1), :] = mj
        l_scr[pl.ds(j, 1), :] = jnp.sum(p, axis=0, keepdims=True)
        pv = _mm(p.T[0:n_pairs, :], vblk)
        acc_scr[pl.ds(pl.multiple_of(j * n_pairs, n_pairs), n_pairs), :] = pv

    @pl.when(jb == pl.num_programs(1) - 1)
    def _():
        gate = lax.dot_general(kmean_scr[...], qbd, (((1,), (1,)), ((), ())), precision=HI,
                               preferred_element_type=F32)
        row = lax.broadcasted_iota(jnp.int32, gate.shape, 0)
        g = gate
        sel = jnp.zeros(gate.shape, jnp.bool_)
        for _ in range(MOBA_TOPK):
            mx = jnp.max(g, axis=0, keepdims=True)
            idx = jnp.min(jnp.where(g == mx, row, n_blocks), axis=0, keepdims=True)
            hit = row == idx
            sel = sel | hit
            g = jnp.where(hit, NEG_INF, g)
        kn = jnp.concatenate([kn_ref[...], jnp.zeros((8 - n_new, GROUP_W), F32)], axis=0)
        s_own = _mm_nt(kn, qbd) * scale
        r8 = lax.broadcasted_iota(jnp.int32, (8, LANES), 0)
        l8 = lax.broadcasted_iota(jnp.int32, (8, LANES), 1)
        dist = (l8 >> 2) - r8
        own_bias = jnp.zeros((8, LANES), F32)
        for h in range(N_HEADS):
            for d in range(n_new):
                own_bias = jnp.where(((l8 & 3) == h) & (dist == d), relb_ref[h, d], own_bias)
        own_ok = (dist >= 0) & (r8 < n_new)
        s_own = jnp.where(own_ok, s_own + own_bias, NEG_INF)
        m_all = jnp.maximum(jnp.max(jnp.where(sel, m_scr[...], NEG_INF), axis=0, keepdims=True),
                            jnp.max(s_own, axis=0, keepdims=True))
        wgt = jnp.where(sel, jnp.exp(m_scr[...] - m_all), 0.0)
        p_own = jnp.exp(s_own - m_all)
        den = jnp.sum(wgt * l_scr[...], axis=0, keepdims=True) + jnp.sum(p_own, axis=0, keepdims=True)
        packed = jnp.concatenate([wgt, p_own, den, jnp.zeros((LANES - n_blocks - 9, LANES), F32)], axis=0)
        pk = packed.T
        out = jnp.zeros((n_pairs, GROUP_W), F32)
        for j in range(n_blocks):
            out = out + pk[0:n_pairs, j:j + 1] * acc_scr[j * n_pairs:(j + 1) * n_pairs, :]
        vn = vn_ref[...]
        for t in range(n_new):
            out = out + pk[0:n_pairs, n_blocks + t:n_blocks + t + 1] * vn[t:t + 1, :]
        out = out / pk[0:n_pairs, n_blocks + 8:n_blocks + 9]
        ph = lax.broadcasted_iota(jnp.int32, (n_pairs, HEAD_DIM), 0) & 3
        o16 = jnp.zeros((n_pairs, HEAD_DIM), F32)
        for h in range(N_HEADS):
            o16 = jnp.where(ph == h, out[:, h * HEAD_DIM:(h + 1) * HEAD_DIM], o16)
        for t in range(n_new):
            for h in range(N_HEADS):
                p = t * N_HEADS + h
                o_ref[t:t + 1, h * HEAD_DIM:(h + 1) * HEAD_DIM] = o16[p:p + 1, :]


def _moba_decode(q, k_new, v_new, cache_k, cache_v, page_table, rel_bias, pages_per_step=8):
    s, n_new, _ = q.shape
    n_pages = page_table.shape[1]
    page = cache_k.shape[1]
    assert MOBA_BLOCK == 2 * page and n_pages % pages_per_step == 0 and pages_per_step % 2 == 0
    n_blocks = n_pages * page // MOBA_BLOCK
    assert n_blocks >= MOBA_TOPK and n_new <= 8
    npg = pages_per_step
    past = n_pages * page
    r = np.arange(MOBA_BLOCK)[:, None]
    lane_tok = (np.arange(LANES) >> 2)[None, :]
    bkt = jnp.asarray(_bucket_np(past + lane_tok - ((n_blocks - 1) * MOBA_BLOCK + r)))
    n_pairs = n_new * N_HEADS

    def page_spec(p):
        return pl.BlockSpec((None, page, GROUP_W), lambda i, j, pt: (pt[i, j * npg + p], 0, 0))

    new_spec = pl.BlockSpec((None, n_new, GROUP_W), lambda i, j, pt: (i, 0, 0))
    kern = functools.partial(_moba_decode_kernel, pages_per_step=npg, n_blocks=n_blocks, n_new=n_new)
    grid_spec = pltpu.PrefetchScalarGridSpec(
        num_scalar_prefetch=1,
        grid=(s, n_pages // npg),
        in_specs=[pl.BlockSpec(memory_space=pltpu.SMEM),
                  pl.BlockSpec((MOBA_BLOCK, LANES), lambda i, j, pt: (0, 0)),
                  new_spec, new_spec, new_spec]
                 + [page_spec(p) for p in range(npg)] + [page_spec(p) for p in range(npg)],
        out_specs=new_spec,
        scratch_shapes=[pltpu.VMEM((LANES, GROUP_W), F32),
                        pltpu.VMEM((n_blocks, LANES), F32), pltpu.VMEM((n_blocks, LANES), F32),
                        pltpu.VMEM((n_blocks * n_pairs, GROUP_W), F32),
                        pltpu.VMEM((n_blocks, GROUP_W), F32),
                        pltpu.VMEM((MOBA_BLOCK, LANES), F32)],
    )
    return pl.pallas_call(
        kern,
        grid_spec=grid_spec,
        out_shape=jax.ShapeDtypeStruct((s, n_new, GROUP_W), F32),
        compiler_params=pltpu.CompilerParams(dimension_semantics=("arbitrary", "arbitrary"),
                                             vmem_limit_bytes=VMEM_LIMIT),
        name="moba_decode",
    )(page_table, rel_bias, bkt, q, k_new, v_new, *([cache_k] * npg), *([cache_v] * npg))


def _out_kernel(x_ref, od_ref, oa_ref, g1_ref, sh2_ref, sc2_ref, g2_ref, n2_ref, nf_ref,
                wo_ref, w1_ref, w2_ref, y_ref, *, ff_chunk):
    x = x_ref[...]
    mix = (jnp.dot(od_ref[...].astype(BF16), wo_ref[0:GROUP_W, :], preferred_element_type=F32)
           + jnp.dot(oa_ref[...].astype(BF16), wo_ref[GROUP_W:2 * GROUP_W, :], preferred_element_type=F32))
    x1 = x + g1_ref[...] * mix
    r = lax.rsqrt(jnp.mean(x1 * x1, axis=-1, keepdims=True) + EPS)
    h = ((x1 * r * n2_ref[...]) * (1.0 + sc2_ref[...]) + sh2_ref[...]).astype(BF16)
    d_ff = w1_ref.shape[1]
    ff = jnp.zeros(x.shape, F32)
    for c in range(d_ff // ff_chunk):
        a = jnp.dot(h, w1_ref[:, c * ff_chunk:(c + 1) * ff_chunk], preferred_element_type=F32)
        a = jnp.square(jnp.maximum(a, 0.0)).astype(BF16)
        ff = ff + jnp.dot(a, w2_ref[c * ff_chunk:(c + 1) * ff_chunk, :], preferred_element_type=F32)
    x2 = x1 + g2_ref[...] * ff
    r2 = lax.rsqrt(jnp.mean(x2 * x2, axis=-1, keepdims=True) + EPS)
    y_ref[...] = x2 * r2 * nf_ref[...]


def _out_mlp(x, o_d, o_a, g1, sh2, sc2, g2, n2, nf, w_out, w1, w2, tm, rows_per_mod, ff_chunk=1024):
    t, d = x.shape
    nt = t // tm
    if g1.shape[1] == 1:
        per = rows_per_mod // tm
        mod_spec = pl.BlockSpec((None, 1, d), lambda i: (i // per, 0, 0))
    else:
        mod_spec = pl.BlockSpec((None, tm, d), lambda i: (i, 0, 0))
    const = lambda i: (0, 0)
    row = lambda w: pl.BlockSpec((tm, w), lambda i: (i, 0))
    kern = functools.partial(_out_kernel, ff_chunk=ff_chunk)
    return pl.pallas_call(
        kern,
        grid=(nt,),
        in_specs=[row(d), row(GROUP_W), row(GROUP_W), mod_spec, mod_spec, mod_spec, mod_spec,
                  pl.BlockSpec((1, d), const), pl.BlockSpec((1, d), const),
                  pl.BlockSpec(w_out.shape, const), pl.BlockSpec(w1.shape, const),
                  pl.BlockSpec(w2.shape, const)],
        out_specs=row(d),
        out_shape=jax.ShapeDtypeStruct((t, d), F32),
        compiler_params=pltpu.CompilerParams(dimension_semantics=("arbitrary",),
                                             vmem_limit_bytes=VMEM_LIMIT),
        name="out_mlp",
    )(x, o_d, o_a, g1, sh2, sc2, g2, n2, nf, w_out, w1, w2)


def _split_w_in(w_in):
    ab0 = CONV_CH + GROUP_W
    w_main = jnp.concatenate([w_in[:, :ab0], w_in[:, ab0 + 2 * N_HEADS:]], axis=1).astype(BF16)
    w_ab = jnp.pad(w_in[:, ab0:ab0 + 2 * N_HEADS], ((0, 0), (0, LANES - 2 * N_HEADS)))
    return w_main, w_ab


def _lane_pad(v):
    return jnp.pad(v.astype(F32), (0, LANES - v.shape[0])).reshape(1, LANES)


def kernel(x_prompt, x_sample, cache_k, cache_v, state_delta, state_conv, page_table, c_prompt, c_sample,
           norm1_w, norm2_w, normf_w, w_ada, b_ada, w_in, w_out, conv_w, a_log, dt_bias, onorm_w,
           rel_bias, w_mlp1, w_mlp2):
    bp, tp, d = x_prompt.shape
    bs, ts, _ = x_sample.shape
    depth = w_in.shape[0]
    assert depth == 1 and d == 2 * GROUP_W
    l = 0
    n_pool, page = cache_k.shape[1], cache_k.shape[2]

    mod = _modulation(jnp.concatenate([c_prompt, c_sample], axis=0), w_ada[l], b_ada[l])
    mod = mod.reshape(bp + bs, 6, d)
    mod_p = [mod[:bp, i].reshape(bp, 1, d) for i in range(6)]
    tm_s = bs * ts
    mod_s = [jnp.broadcast_to(mod[bp:, i][:, None, :], (bs, ts, d)).reshape(1, tm_s, d) for i in range(6)]

    w_main, w_ab = _split_w_in(w_in[l])
    alog, dtb = _lane_pad(a_log[l]), _lane_pad(dt_bias[l])
    n1 = norm1_w[l].reshape(1, d)
    n2 = norm2_w[l].reshape(1, d)
    nf = normf_w.reshape(1, d)
    wo = w_out[l].astype(BF16)
    w1 = w_mlp1[l].astype(BF16)
    w2 = w_mlp2[l].astype(BF16)
    cw = conv_w[l]
    onw = onorm_w[l].reshape(1, HEAD_DIM)

    tm = 512
    xp = x_prompt.reshape(bp * tp, d)
    qkv_p, z_p, qa_p, ka_p, va_p, gb_p, gbt_p = _inproj(
        xp, mod_p[0], mod_p[1], n1, w_main, w_ab, alog, dtb, tm, tp)
    n_chunks = 4
    od_p, s_p = _gdn(qkv_p.reshape(bp, tp, CONV_CH), gb_p.reshape(bp, tp, LANES),
                     gbt_p.reshape(8, bp, tp).transpose(1, 0, 2), z_p.reshape(bp, tp, GROUP_W),
                     jnp.zeros((bp, CONV_W - 1, CONV_CH), F32),
                     jnp.zeros((bp, N_HEADS, HEAD_DIM, HEAD_DIM), F32), cw, onw, n_chunks)
    oa_p = _moba_prompt(qa_p.reshape(bp, tp, GROUP_W), ka_p.reshape(bp, tp, GROUP_W),
                        va_p.reshape(bp, tp, GROUP_W), rel_bias)
    y_p = _out_mlp(xp, od_p.reshape(bp * tp, GROUP_W), oa_p.reshape(bp * tp, GROUP_W),
                   mod_p[2], mod_p[3], mod_p[4], mod_p[5], n2, nf, wo, w1, w2, tm, tp)

    xs = x_sample.reshape(tm_s, d)
    qkv_s, z_s, qa_s, ka_s, va_s, gb_s, gbt_s = _inproj(
        xs, mod_s[0], mod_s[1], n1, w_main, w_ab, alog, dtb, tm_s, tm_s)
    gb_s3 = jnp.pad(gb_s.reshape(bs, ts, LANES), ((0, 0), (0, CHUNK - ts), (0, 0)))
    gbt_s3 = jnp.pad(gbt_s.reshape(8, bs, ts).transpose(1, 0, 2), ((0, 0), (0, 0), (0, CHUNK - ts)))
    od_s, s_s = _gdn(qkv_s.reshape(bs, ts, CONV_CH), gb_s3, gbt_s3, z_s.reshape(bs, ts, GROUP_W),
                     state_conv[l], state_delta[l], cw, onw, 1)
    oa_s = _moba_decode(qa_s.reshape(bs, ts, GROUP_W), ka_s.reshape(bs, ts, GROUP_W),
                        va_s.reshape(bs, ts, GROUP_W),
                        cache_k[l].reshape(n_pool, page, GROUP_W), cache_v[l].reshape(n_pool, page, GROUP_W),
                        page_table, rel_bias)
    y_s = _out_mlp(xs, od_s.reshape(tm_s, GROUP_W), oa_s.reshape(tm_s, GROUP_W),
                   mod_s[2], mod_s[3], mod_s[4], mod_s[5], n2, nf, wo, w1, w2, tm_s, tm_s)

    shp_p = (1, bp, tp, N_HEADS, HEAD_DIM)
    shp_s = (1, bs, ts, N_HEADS, HEAD_DIM)
    qkv_p3 = qkv_p.reshape(bp, tp, CONV_CH)
    qkv_s3 = qkv_s.reshape(bs, ts, CONV_CH)
    conv_s = jnp.concatenate([state_conv[l], qkv_s3], axis=1)[:, ts:]
    return (y_p.reshape(bp, tp, d), y_s.reshape(bs, ts, d),
            ka_p.reshape(shp_p), va_p.reshape(shp_p), s_p[None], qkv_p3[:, tp - (CONV_W - 1):][None],
            ka_s.reshape(shp_s), va_s.reshape(shp_s), s_s[None], conv_s[None])
```

```python
import functools
import math

import numpy as np
import jax
import jax.numpy as jnp
from jax import lax
from jax.experimental import pallas as pl
from jax.experimental.pallas import tpu as pltpu

F32 = jnp.float32
BF16 = jnp.bfloat16
HI = lax.Precision.HIGHEST
NEG_INF = float("-inf")

EPS = 1e-6
N_HEADS = 4
HEAD_DIM = 128
GROUP_W = N_HEADS * HEAD_DIM
CONV_W = 4
CONV_CH = 3 * GROUP_W
CHUNK = 64
STACK = N_HEADS * CHUNK
MOBA_BLOCK = 256
MOBA_TOPK = 3
NUM_BUCKETS = 32
MAX_DISTANCE = 128
LANES = 128
VMEM_LIMIT = 56 * 1024 * 1024


def _bucket_np(dist):
    n = np.maximum(dist, 0)
    max_exact = NUM_BUCKETS // 2
    nf = np.maximum(n, 1).astype(np.float32)
    large = max_exact + (np.log(nf / max_exact) / math.log(MAX_DISTANCE / max_exact)
                         * (NUM_BUCKETS - max_exact)).astype(np.int32)
    large = np.minimum(large, NUM_BUCKETS - 1)
    return np.where(n < max_exact, n, large).astype(np.int32)


def _mm(a, b):
    return jnp.dot(a.astype(BF16), b.astype(BF16), preferred_element_type=F32)


def _mm_nt(a, b):
    return lax.dot_general(a.astype(BF16), b.astype(BF16), (((1,), (1,)), ((), ())),
                           preferred_element_type=F32)


def _mm_tn(a, b):
    return lax.dot_general(a.astype(BF16), b.astype(BF16), (((0,), (0,)), ((), ())),
                           preferred_element_type=F32)


def _silu(x):
    return x * jax.nn.sigmoid(x)


def _softplus(x):
    return jnp.maximum(x, 0.0) + jnp.log1p(jnp.exp(-jnp.abs(x)))


def _mod_kernel(c_ref, w_ref, b_ref, o_ref):
    o_ref[...] = jnp.dot(_silu(c_ref[...]), w_ref[...], precision=HI,
                         preferred_element_type=F32) + b_ref[...]


def _modulation(c, w, b, tn=1536):
    m, d = c.shape
    n = w.shape[1]
    return pl.pallas_call(
        _mod_kernel,
        grid=(n // tn,),
        in_specs=[pl.BlockSpec((m, d), lambda j: (0, 0)),
                  pl.BlockSpec((d, tn), lambda j: (0, j)),
                  pl.BlockSpec((1, tn), lambda j: (0, j))],
        out_specs=pl.BlockSpec((m, tn), lambda j: (0, j)),
        out_shape=jax.ShapeDtypeStruct((m, n), F32),
        compiler_params=pltpu.CompilerParams(dimension_semantics=("arbitrary",),
                                             vmem_limit_bytes=VMEM_LIMIT),
        name="modulation",
    )(c, w, b.reshape(1, n))


def _inproj_kernel(x_ref, sh_ref, sc_ref, n1_ref, wm_ref, wab_ref, alog_ref, dtb_ref,
                   qkv_ref, z_ref, qa_ref, ka_ref, va_ref, gb_ref, gbt_ref):
    x = x_ref[...]
    r = lax.rsqrt(jnp.mean(x * x, axis=-1, keepdims=True) + EPS)
    h = (x * r * n1_ref[...]) * (1.0 + sc_ref[...]) + sh_ref[...]
    hb = h.astype(BF16)
    off = 0
    for ref in (qkv_ref, z_ref, qa_ref, ka_ref, va_ref):
        w = ref.shape[-1]
        ref[...] = jnp.dot(hb, wm_ref[:, off:off + w], preferred_element_type=F32)
        off += w
    ab = jnp.dot(h, wab_ref[...], precision=HI, preferred_element_type=F32)
    lane = lax.broadcasted_iota(jnp.int32, ab.shape, 1)
    g = -jnp.exp(alog_ref[...]) * _softplus(ab + dtb_ref[...])
    gb = jnp.where(lane < N_HEADS, g, jax.nn.sigmoid(ab))
    gb_ref[...] = gb
    gbt_ref[...] = gb.T[0:8, :]


def _inproj(x, shift, scale, n1, w_main, w_ab, alog, dtb, tm, rows_per_mod):
    t, d = x.shape
    nt = t // tm
    if shift.shape[1] == 1:
        per = rows_per_mod // tm
        mod_spec = pl.BlockSpec((None, 1, d), lambda i: (i // per, 0, 0))
    else:
        mod_spec = pl.BlockSpec((None, tm, d), lambda i: (i, 0, 0))
    const = lambda i: (0, 0)
    widths = (CONV_CH, GROUP_W, GROUP_W, GROUP_W, GROUP_W)
    outs = [jax.ShapeDtypeStruct((t, w), F32) for w in widths]
    outs += [jax.ShapeDtypeStruct((t, LANES), F32), jax.ShapeDtypeStruct((8, t), F32)]
    out_specs = [pl.BlockSpec((tm, w), lambda i: (i, 0)) for w in widths]
    out_specs += [pl.BlockSpec((tm, LANES), lambda i: (i, 0)), pl.BlockSpec((8, tm), lambda i: (0, i))]
    return pl.pallas_call(
        _inproj_kernel,
        grid=(nt,),
        in_specs=[pl.BlockSpec((tm, d), lambda i: (i, 0)), mod_spec, mod_spec,
                  pl.BlockSpec((1, d), const),
                  pl.BlockSpec(w_main.shape, const),
                  pl.BlockSpec(w_ab.shape, const),
                  pl.BlockSpec((1, LANES), const), pl.BlockSpec((1, LANES), const)],
        out_specs=out_specs,
        out_shape=outs,
        compiler_params=pltpu.CompilerParams(dimension_semantics=("arbitrary",),
                                             vmem_limit_bytes=VMEM_LIMIT),
        name="inproj",
    )(x, shift, scale, n1, w_main, w_ab, alog, dtb)


def _gdn_kernel(x_ref, gb_ref, gbt_ref, z_ref, cs_ref, s0_ref, cw_ref, onw_ref,
                o_ref, sn_ref, xbuf, s_scr, *, t_real, n_chunks, n_seq):
    ti = pl.program_id(1)
    t_tile = n_chunks * CHUNK

    @pl.when(ti == 0)
    def _():
        for n in range(n_seq):
            xbuf[n, 0:8, :] = jnp.zeros((8, CONV_CH), F32)
            xbuf[n, 5:8, :] = cs_ref[n]
            if t_real < t_tile:
                xbuf[n, 8:8 + t_tile, :] = jnp.zeros((t_tile, CONV_CH), F32)
        s_scr[...] = s0_ref[...]

    for n in range(n_seq):
        xbuf[n, 8:8 + t_real, :] = x_ref[n]

    ri = lax.broadcasted_iota(jnp.int32, (STACK, STACK), 0)
    ci = lax.broadcasted_iota(jnp.int32, (STACK, STACK), 1)
    same = (ri >> 6) == (ci >> 6)
    causal = same & ((ri & 63) >= (ci & 63))
    strict = same & ((ri & 63) > (ci & 63))
    eye = (ri == ci).astype(F32)
    l_st = (lax.broadcasted_iota(jnp.int32, (STACK, CHUNK), 1)
            <= (lax.broadcasted_iota(jnp.int32, (STACK, CHUNK), 0) & 63)).astype(F32)
    u_st = (lax.broadcasted_iota(jnp.int32, (CHUNK, STACK), 0)
            <= (lax.broadcasted_iota(jnp.int32, (CHUNK, STACK), 1) & 63)).astype(F32)
    row_head = lax.broadcasted_iota(jnp.int32, (STACK, LANES), 0) >> 6
    lane_id = lax.broadcasted_iota(jnp.int32, (STACK, LANES), 1)
    pick_g = lane_id == row_head
    pick_b = lane_id == row_head + N_HEADS
    col_head8 = lax.broadcasted_iota(jnp.int32, (8, STACK), 1) >> 6
    sub8 = lax.broadcasted_iota(jnp.int32, (8, STACK), 0)
    cw = cw_ref[...]
    onw = onw_ref[...]

    def stack(a):
        return jnp.concatenate([a[:, h * HEAD_DIM:(h + 1) * HEAD_DIM] for h in range(N_HEADS)], axis=0)

    def l2n(a):
        return a * lax.rsqrt(jnp.sum(a * a, axis=-1, keepdims=True) + EPS)

    items = [(n, c) for n in range(n_seq) for c in range(n_chunks)]

    def conv_qkv(n, c):
        base = 8 + c * CHUNK
        y = xbuf[n, base - 3:base - 3 + CHUNK, :] * cw[0:1, :]
        for i in range(1, CONV_W):
            y = y + xbuf[n, base - 3 + i:base - 3 + i + CHUNK, :] * cw[i:i + 1, :]
        y = _silu(y)
        q = l2n(stack(y[:, 0:GROUP_W])) * (HEAD_DIM ** -0.5)
        k = l2n(stack(y[:, GROUP_W:2 * GROUP_W]))
        v = stack(y[:, 2 * GROUP_W:3 * GROUP_W])
        return q, k, v

    def gates(n, c):
        gb = gb_ref[n, c * CHUNK:(c + 1) * CHUNK, :]
        gbt = gbt_ref[n, :, c * CHUNK:(c + 1) * CHUNK]
        gcum = jnp.dot(l_st, gb, precision=HI, preferred_element_type=F32)
        g_col = jnp.sum(jnp.where(pick_g, gcum, 0.0), axis=1, keepdims=True)
        g_last = gcum[CHUNK - 1:CHUNK, :]
        g_tot = jnp.sum(jnp.where(pick_g, jnp.broadcast_to(g_last, (STACK, LANES)), 0.0),
                        axis=1, keepdims=True)
        gb_st = jnp.concatenate([gb] * N_HEADS, axis=0)
        beta = jnp.sum(jnp.where(pick_b, gb_st, 0.0), axis=1, keepdims=True)
        grow_all = jnp.dot(gbt, u_st, precision=HI, preferred_element_type=F32)
        g_row = jnp.sum(jnp.where(sub8 == col_head8, grow_all, 0.0), axis=0, keepdims=True)
        decay = jnp.exp(jnp.where(causal, g_col - g_row, NEG_INF))
        return g_col, g_tot, g_last, beta, decay

    qkv = [conv_qkv(n, c) for n, c in items]
    gts = [gates(n, c) for n, c in items]
    kbs = [k.astype(BF16) for _, k, _ in qkv]
    a_mats = [jnp.where(strict, g[3] * _mm_nt(kb, kb) * g[4], 0.0) for kb, g in zip(kbs, gts)]
    invs = [eye - a for a in a_mats]
    pws = a_mats
    for _ in range(5):
        pws = [_mm(p, p) for p in pws]
        invs = [x + _mm(x, p) for x, p in zip(invs, pws)]
    resids = [eye - jnp.dot(eye + a, x, precision=HI, preferred_element_type=F32) for a, x in zip(a_mats, invs)]
    invs = [x + _mm(x, r) for x, r in zip(invs, resids)]
    exp_gs = [jnp.exp(g[0]) for g in gts]
    sols = [_mm(x, jnp.concatenate([v * g[3], k * (g[3] * eg)], axis=1))
            for x, (_, k, v), g, eg in zip(invs, qkv, gts, exp_gs)]
    qks = [_mm_nt(q, kb) * g[4] for (q, _, _), kb, g in zip(qkv, kbs, gts)]
    q_decs = [q * eg for (q, _, _), eg in zip(qkv, exp_gs)]
    k_tails = [k * jnp.exp(g[1] - g[0]) for (_, k, _), g in zip(qkv, gts)]

    for idx, (n, c) in enumerate(items):
        u0 = sols[idx][:, 0:HEAD_DIM]
        w = sols[idx][:, HEAD_DIM:2 * HEAD_DIM]
        us, qs = [], []
        for h in range(N_HEADS):
            sl = slice(h * CHUNK, (h + 1) * CHUNK)
            wq = jnp.concatenate([w[sl], q_decs[idx][sl]], axis=0)
            r = _mm(wq, s_scr[n, h])
            us.append(u0[sl] - r[0:CHUNK])
            qs.append(r[CHUNK:2 * CHUNK])
        u = jnp.concatenate(us, axis=0)
        o = jnp.concatenate(qs, axis=0) + _mm(qks[idx], u)
        for h in range(N_HEADS):
            sl = slice(h * CHUNK, (h + 1) * CHUNK)
            gl = jnp.exp(gts[idx][2][:, h:h + 1])
            s_scr[n, h] = s_scr[n, h] * gl + _mm_tn(k_tails[idx][sl], u[sl])

        o = o * lax.rsqrt(jnp.mean(o * o, axis=-1, keepdims=True) + EPS) * onw
        rows = min(CHUNK, t_real - c * CHUNK)
        for h in range(N_HEADS):
            zg = _silu(z_ref[n, c * CHUNK:c * CHUNK + rows, h * HEAD_DIM:(h + 1) * HEAD_DIM])
            o_ref[n, c * CHUNK:c * CHUNK + rows, h * HEAD_DIM:(h + 1) * HEAD_DIM] = \
                o[h * CHUNK:h * CHUNK + rows] * zg

    if t_real == t_tile:
        for n in range(n_seq):
            xbuf[n, 5:8, :] = xbuf[n, 8 + t_tile - 3:8 + t_tile, :]

    @pl.when(ti == pl.num_programs(1) - 1)
    def _():
        sn_ref[...] = s_scr[...]


def _gdn(qkv, gb, gbt, z, conv_state, s0, conv_w, onorm_w, n_chunks, n_seq):
    b, t, _ = qkv.shape
    t_tile = n_chunks * CHUNK
    t_real = min(t, t_tile)
    nt = max(1, t // t_tile)
    assert b % n_seq == 0
    kern = functools.partial(_gdn_kernel, t_real=t_real, n_chunks=n_chunks, n_seq=n_seq)
    return pl.pallas_call(
        kern,
        grid=(b // n_seq, nt),
        in_specs=[pl.BlockSpec((n_seq, t_real, CONV_CH), lambda i, j: (i, j, 0)),
                  pl.BlockSpec((n_seq, t_tile, LANES), lambda i, j: (i, j, 0)),
                  pl.BlockSpec((n_seq, 8, t_tile), lambda i, j: (i, 0, j)),
                  pl.BlockSpec((n_seq, t_real, GROUP_W), lambda i, j: (i, j, 0)),
                  pl.BlockSpec((n_seq, CONV_W - 1, CONV_CH), lambda i, j: (i, 0, 0)),
                  pl.BlockSpec((n_seq, N_HEADS, HEAD_DIM, HEAD_DIM), lambda i, j: (i, 0, 0, 0)),
                  pl.BlockSpec((CONV_W, CONV_CH), lambda i, j: (0, 0)),
                  pl.BlockSpec((1, HEAD_DIM), lambda i, j: (0, 0))],
        out_specs=[pl.BlockSpec((n_seq, t_real, GROUP_W), lambda i, j: (i, j, 0)),
                   pl.BlockSpec((n_seq, N_HEADS, HEAD_DIM, HEAD_DIM), lambda i, j: (i, 0, 0, 0))],
        out_shape=[jax.ShapeDtypeStruct((b, t, GROUP_W), F32),
                   jax.ShapeDtypeStruct((b, N_HEADS, HEAD_DIM, HEAD_DIM), F32)],
        scratch_shapes=[pltpu.VMEM((n_seq, 8 + t_tile, CONV_CH), F32),
                        pltpu.VMEM((n_seq, N_HEADS, HEAD_DIM, HEAD_DIM), F32)],
        compiler_params=pltpu.CompilerParams(dimension_semantics=("arbitrary", "arbitrary"),
                                             vmem_limit_bytes=VMEM_LIMIT),
        name="gated_delta",
    )(qkv, gb, gbt, z, conv_state, s0, conv_w, onorm_w)


def _bias_from_buckets(bkt, relb_ref, h):
    def body(t, b):
        return jnp.where(bkt == t, relb_ref[h, t], b)
    return lax.fori_loop(0, NUM_BUCKETS, body, jnp.zeros(bkt.shape, F32))


def _moba_prompt_kernel(relb_ref, bkt_ref, q_ref, k_ref, v_ref, o_ref,
                        kb_scr, vb_scr, kmean_scr, bias_scr, *, n_blocks, group):
    h = pl.program_id(1)
    qi = pl.program_id(2)
    blk = MOBA_BLOCK
    scale = HEAD_DIM ** -0.5

    @pl.when(qi == 0)
    def _():
        kb_scr[...] = k_ref[...].astype(BF16)
        vb_scr[...] = v_ref[...].astype(BF16)
        for j in range(n_blocks):
            kmean_scr[j:j + 1, :] = jnp.mean(k_ref[j * blk:(j + 1) * blk, :], axis=0, keepdims=True)
        bias_scr[...] = _bias_from_buckets(bkt_ref[...], relb_ref, h)

    q = q_ref[...]
    qb = q.astype(BF16)
    far_bias = relb_ref[h, NUM_BUCKETS - 1]

    gate = lax.dot_general(kmean_scr[...], q, (((1,), (1,)), ((), ())), precision=HI,
                           preferred_element_type=F32)
    brow = lax.broadcasted_iota(jnp.int32, gate.shape, 0)
    g = jnp.where(brow < qi, gate, NEG_INF)
    sel_t = jnp.zeros(gate.shape, F32)
    for s in range(MOBA_TOPK):
        mx = jnp.max(g, axis=0, keepdims=True)
        idx = jnp.min(jnp.where(g == mx, brow, n_blocks), axis=0, keepdims=True)
        hit = brow == idx
        slot_ok = jnp.where(qi > s, 1.0, 0.0)
        sel_t = jnp.maximum(sel_t, jnp.where(hit, slot_ok, 0.0))
        g = jnp.where(hit, NEG_INF, g)
    sel = jnp.concatenate([sel_t, jnp.zeros((LANES - n_blocks, blk), F32)], axis=0).T
    lane = lax.broadcasted_iota(jnp.int32, sel.shape, 1)
    prev_chosen = jnp.sum(jnp.where(lane == qi - 1, sel, 0.0), axis=1, keepdims=True) > 0.0
    sel_far = jnp.where(lane == qi - 1, 0.0, sel)

    ri = lax.broadcasted_iota(jnp.int32, (blk, blk), 0)
    ci = lax.broadcasted_iota(jnp.int32, (blk, blk), 1)
    own0 = pl.multiple_of(qi * blk, blk)
    prev0 = pl.multiple_of(jnp.maximum(qi - 1, 0) * blk, blk)
    kd = jnp.concatenate([kb_scr[pl.ds(prev0, blk), :], kb_scr[pl.ds(own0, blk), :]], axis=0)
    vd = jnp.concatenate([vb_scr[pl.ds(prev0, blk), :], vb_scr[pl.ds(own0, blk), :]], axis=0)
    s0 = _mm_nt(qb, kd) * scale + bias_scr[...]
    keep = jnp.concatenate([jnp.broadcast_to(prev_chosen, (blk, blk)), ri >= ci], axis=1)
    s0 = jnp.where(keep, s0, NEG_INF)
    m0 = jnp.max(s0, axis=1, keepdims=True)
    p0 = jnp.exp(s0 - m0)
    l0 = jnp.sum(p0, axis=1, keepdims=True)
    acc0 = _mm(p0, vd)

    def body(i, carry):
        m, l, acc = carry
        start = pl.multiple_of(i * (group * blk), group * blk)
        s = _mm_nt(qb, kb_scr[pl.ds(start, group * blk), :]) * scale + far_bias
        parts = []
        for u in range(group):
            chosen = jnp.sum(jnp.where(lane == i * group + u, sel_far, 0.0), axis=1, keepdims=True) > 0.0
            parts.append(jnp.where(chosen, s[:, u * blk:(u + 1) * blk], NEG_INF))
        s = jnp.concatenate(parts, axis=1)
        m_new = jnp.maximum(m, jnp.max(s, axis=1, keepdims=True))
        alpha = jnp.exp(m - m_new)
        p = jnp.exp(s - m_new)
        l = alpha * l + jnp.sum(p, axis=1, keepdims=True)
        acc = alpha * acc + _mm(p, vb_scr[pl.ds(start, group * blk), :])
        return m_new, l, acc

    m, l, acc = lax.fori_loop(0, (qi + group - 2) // group, body, (m0, l0, acc0))
    o_ref[...] = acc / l


def _moba_prompt(q, k, v, rel_bias, group=4):
    b, t, _ = q.shape
    blk = MOBA_BLOCK
    nb = t // blk
    r = np.arange(blk)[:, None]
    c = np.arange(blk)[None, :]
    bkt = jnp.asarray(np.concatenate([_bucket_np(blk + r - c), _bucket_np(r - c)], axis=1))
    assert nb % group == 0 and nb <= LANES
    kern = functools.partial(_moba_prompt_kernel, n_blocks=nb, group=group)
    return pl.pallas_call(
        kern,
        grid=(b, N_HEADS, nb),
        in_specs=[pl.BlockSpec(memory_space=pltpu.SMEM),
                  pl.BlockSpec((blk, 2 * blk), lambda i, h, j: (0, 0)),
                  pl.BlockSpec((None, blk, HEAD_DIM), lambda i, h, j: (i, j, h)),
                  pl.BlockSpec((None, t, HEAD_DIM), lambda i, h, j: (i, 0, h)),
                  pl.BlockSpec((None, t, HEAD_DIM), lambda i, h, j: (i, 0, h))],
        out_specs=pl.BlockSpec((None, blk, HEAD_DIM), lambda i, h, j: (i, j, h)),
        out_shape=jax.ShapeDtypeStruct((b, t, GROUP_W), F32),
        scratch_shapes=[pltpu.VMEM((t, HEAD_DIM), BF16), pltpu.VMEM((t, HEAD_DIM), BF16),
                        pltpu.VMEM((nb, HEAD_DIM), F32), pltpu.VMEM((blk, 2 * blk), F32)],
        compiler_params=pltpu.CompilerParams(dimension_semantics=("arbitrary", "arbitrary", "arbitrary"),
                                             vmem_limit_bytes=VMEM_LIMIT),
        name="moba_prompt",
    )(rel_bias, bkt, q, k, v)


def _moba_decode_kernel(pt_ref, relb_ref, bkt_ref, q_ref, kn_ref, vn_ref, *rest,
                        pages_per_step, n_blocks, n_new):
    del pt_ref
    npg = pages_per_step
    k_refs = rest[0:npg]
    v_refs = rest[npg:2 * npg]
    o_ref = rest[2 * npg]
    qbd_scr, m_scr, l_scr, acc_scr, kmean_scr, bias_scr = rest[2 * npg + 1:]
    jb = pl.program_id(1)
    blk = MOBA_BLOCK
    n_pairs = n_new * N_HEADS
    scale = HEAD_DIM ** -0.5
    bps = npg // 2

    lane1 = lax.broadcasted_iota(jnp.int32, (1, LANES), 1)
    far_row = jnp.zeros((1, LANES), F32)
    for h in range(N_HEADS):
        far_row = jnp.where((lane1 & 3) == h, relb_ref[h, NUM_BUCKETS - 1], far_row)

    @pl.when(jb == 0)
    def _():
        q = q_ref[...]
        rows = jnp.concatenate([jnp.broadcast_to(q[t:t + 1, :], (N_HEADS, GROUP_W)) for t in range(n_new)], axis=0)
        rp = lax.broadcasted_iota(jnp.int32, (n_pairs, GROUP_W), 0)
        cp = lax.broadcasted_iota(jnp.int32, (n_pairs, GROUP_W), 1)
        qbd = jnp.where((rp & 3) == (cp >> 7), rows, 0.0)
        qbd_scr[...] = jnp.zeros(qbd_scr.shape, F32)
        qbd_scr[0:n_pairs, :] = qbd
        bkt = bkt_ref[...]
        lane_h = lax.broadcasted_iota(jnp.int32, bkt.shape, 1) & 3
        b = jnp.zeros(bkt.shape, F32)
        for h in range(N_HEADS):
            b = jnp.where(lane_h == h, _bias_from_buckets(bkt, relb_ref, h), b)
        bias_scr[...] = b

    def load_page(ref):
        page = ref.shape[0] // N_HEADS
        return jnp.concatenate([ref[pl.ds(h, page, stride=N_HEADS), :] for h in range(N_HEADS)], axis=1)

    qbd = qbd_scr[...]
    j0 = pl.multiple_of(jb * bps, bps)
    k_all = jnp.concatenate([load_page(r) for r in k_refs], axis=0)
    kmean_scr[pl.ds(j0, bps), :] = jnp.mean(k_all.reshape(bps, blk, GROUP_W), axis=1)
    s = (_mm_nt(k_all, qbd) * scale).reshape(bps, blk, LANES)
    last_bias = jnp.where(jb == pl.num_programs(1) - 1, bias_scr[...], far_row)
    s = jnp.concatenate([s[0:bps - 1] + far_row, s[bps - 1:bps] + last_bias], axis=0)
    mj = jnp.max(s, axis=1, keepdims=True)
    p = jnp.exp(s - mj)
    m_scr[pl.ds(j0, bps), :] = mj.reshape(bps, LANES)
    l_scr[pl.ds(j0, bps), :] = jnp.sum(p, axis=1)
    p_t = p.reshape(bps * blk, LANES).T[0:n_pairs, :].astype(BF16)
    pvs = []
    for i in range(bps):
        vblk = jnp.concatenate([load_page(v_refs[2 * i]), load_page(v_refs[2 * i + 1])], axis=0)
        pvs.append(_mm(p_t[:, i * blk:(i + 1) * blk], vblk))
    acc_scr[pl.ds(pl.multiple_of(j0 * n_pairs, bps * n_pairs), bps * n_pairs), :] = jnp.concatenate(pvs, axis=0)

    @pl.when(jb == pl.num_programs(1) - 1)
    def _():
        gate = lax.dot_general(kmean_scr[...], qbd, (((1,), (1,)), ((), ())), precision=HI,
                               preferred_element_type=F32)
        row = lax.broadcasted_iota(jnp.int32, gate.shape, 0)
        g = gate
        sel = jnp.zeros(gate.shape, jnp.bool_)
        for _ in range(MOBA_TOPK):
            mx = jnp.max(g, axis=0, keepdims=True)
            idx = jnp.min(jnp.where(g == mx, row, n_blocks), axis=0, keepdims=True)
            hit = row == idx
            sel = sel | hit
            g = jnp.where(hit, NEG_INF, g)
        kn = jnp.concatenate([kn_ref[...], jnp.zeros((8 - n_new, GROUP_W), F32)], axis=0)
        s_own = _mm_nt(kn, qbd) * scale
        r8 = lax.broadcasted_iota(jnp.int32, (8, LANES), 0)
        l8 = lax.broadcasted_iota(jnp.int32, (8, LANES), 1)
        dist = (l8 >> 2) - r8
        own_bias = jnp.zeros((8, LANES), F32)
        for h in range(N_HEADS):
            for d in range(n_new):
                own_bias = jnp.where(((l8 & 3) == h) & (dist == d), relb_ref[h, d], own_bias)
        own_ok = (dist >= 0) & (r8 < n_new)
        s_own = jnp.where(own_ok, s_own + own_bias, NEG_INF)
        m_all = jnp.maximum(jnp.max(jnp.where(sel, m_scr[...], NEG_INF), axis=0, keepdims=True),
                            jnp.max(s_own, axis=0, keepdims=True))
        wgt = jnp.where(sel, jnp.exp(m_scr[...] - m_all), 0.0)
        p_own = jnp.exp(s_own - m_all)
        den = jnp.sum(wgt * l_scr[...], axis=0, keepdims=True) + jnp.sum(p_own, axis=0, keepdims=True)
        packed = jnp.concatenate([wgt, p_own, den, jnp.zeros((LANES - n_blocks - 9, LANES), F32)], axis=0)
        pk = packed.T
        out = jnp.zeros((n_pairs, GROUP_W), F32)
        for j in range(n_blocks):
            out = out + pk[0:n_pairs, j:j + 1] * acc_scr[j * n_pairs:(j + 1) * n_pairs, :]
        vn = vn_ref[...]
        for t in range(n_new):
            out = out + pk[0:n_pairs, n_blocks + t:n_blocks + t + 1] * vn[t:t + 1, :]
        out = out / pk[0:n_pairs, n_blocks + 8:n_blocks + 9]
        ph = lax.broadcasted_iota(jnp.int32, (n_pairs, HEAD_DIM), 0) & 3
        o16 = jnp.zeros((n_pairs, HEAD_DIM), F32)
        for h in range(N_HEADS):
            o16 = jnp.where(ph == h, out[:, h * HEAD_DIM:(h + 1) * HEAD_DIM], o16)
        for t in range(n_new):
            for h in range(N_HEADS):
                p = t * N_HEADS + h
                o_ref[t:t + 1, h * HEAD_DIM:(h + 1) * HEAD_DIM] = o16[p:p + 1, :]


def _moba_decode(q, k_new, v_new, cache_k, cache_v, page_table, rel_bias, pages_per_step=16):
    s, n_new, _ = q.shape
    n_pages = page_table.shape[1]
    page = cache_k.shape[1] // N_HEADS
    assert MOBA_BLOCK == 2 * page and n_pages % pages_per_step == 0 and pages_per_step % 16 == 0
    n_blocks = n_pages * page // MOBA_BLOCK
    assert n_blocks >= MOBA_TOPK and n_new <= 8
    npg = pages_per_step
    past = n_pages * page
    r = np.arange(MOBA_BLOCK)[:, None]
    lane_tok = (np.arange(LANES) >> 2)[None, :]
    bkt = jnp.asarray(_bucket_np(past + lane_tok - ((n_blocks - 1) * MOBA_BLOCK + r)))
    n_pairs = n_new * N_HEADS

    def page_spec(p):
        return pl.BlockSpec((None, page * N_HEADS, HEAD_DIM), lambda i, j, pt: (pt[i, j * npg + p], 0, 0))

    new_spec = pl.BlockSpec((None, n_new, GROUP_W), lambda i, j, pt: (i, 0, 0))
    kern = functools.partial(_moba_decode_kernel, pages_per_step=npg, n_blocks=n_blocks, n_new=n_new)
    grid_spec = pltpu.PrefetchScalarGridSpec(
        num_scalar_prefetch=1,
        grid=(s, n_pages // npg),
        in_specs=[pl.BlockSpec(memory_space=pltpu.SMEM),
                  pl.BlockSpec((MOBA_BLOCK, LANES), lambda i, j, pt: (0, 0)),
                  new_spec, new_spec, new_spec]
                 + [page_spec(p) for p in range(npg)] + [page_spec(p) for p in range(npg)],
        out_specs=new_spec,
        scratch_shapes=[pltpu.VMEM((LANES, GROUP_W), F32),
                        pltpu.VMEM((n_blocks, LANES), F32), pltpu.VMEM((n_blocks, LANES), F32),
                        pltpu.VMEM((n_blocks * n_pairs, GROUP_W), F32),
                        pltpu.VMEM((n_blocks, GROUP_W), F32),
                        pltpu.VMEM((MOBA_BLOCK, LANES), F32)],
    )
    return pl.pallas_call(
        kern,
        grid_spec=grid_spec,
        out_shape=jax.ShapeDtypeStruct((s, n_new, GROUP_W), F32),
        compiler_params=pltpu.CompilerParams(dimension_semantics=("arbitrary", "arbitrary"),
                                             vmem_limit_bytes=VMEM_LIMIT),
        name="moba_decode",
    )(page_table, rel_bias, bkt, q, k_new, v_new, *([cache_k] * npg), *([cache_v] * npg))


def _out_kernel(x_ref, od_ref, oa_ref, g1_ref, sh2_ref, sc2_ref, g2_ref, n2_ref, nf_ref,
                wo_ref, w1_ref, w2_ref, y_ref, *, ff_chunk):
    x = x_ref[...]
    mix = (jnp.dot(od_ref[...].astype(BF16), wo_ref[0:GROUP_W, :], preferred_element_type=F32)
           + jnp.dot(oa_ref[...].astype(BF16), wo_ref[GROUP_W:2 * GROUP_W, :], preferred_element_type=F32))
    x1 = x + g1_ref[...] * mix
    r = lax.rsqrt(jnp.mean(x1 * x1, axis=-1, keepdims=True) + EPS)
    h = ((x1 * r * n2_ref[...]) * (1.0 + sc2_ref[...]) + sh2_ref[...]).astype(BF16)
    d_ff = w1_ref.shape[1]
    ff = jnp.zeros(x.shape, F32)
    for c in range(d_ff // ff_chunk):
        a = jnp.dot(h, w1_ref[:, c * ff_chunk:(c + 1) * ff_chunk], preferred_element_type=F32)
        a = jnp.square(jnp.maximum(a, 0.0)).astype(BF16)
        ff = ff + jnp.dot(a, w2_ref[c * ff_chunk:(c + 1) * ff_chunk, :], preferred_element_type=F32)
    x2 = x1 + g2_ref[...] * ff
    r2 = lax.rsqrt(jnp.mean(x2 * x2, axis=-1, keepdims=True) + EPS)
    y_ref[...] = x2 * r2 * nf_ref[...]


def _out_mlp(x, o_d, o_a, g1, sh2, sc2, g2, n2, nf, w_out, w1, w2, tm, rows_per_mod, ff_chunk=1024):
    t, d = x.shape
    nt = t // tm
    if g1.shape[1] == 1:
        per = rows_per_mod // tm
        mod_spec = pl.BlockSpec((None, 1, d), lambda i: (i // per, 0, 0))
    else:
        mod_spec = pl.BlockSpec((None, tm, d), lambda i: (i, 0, 0))
    const = lambda i: (0, 0)
    row = lambda w: pl.BlockSpec((tm, w), lambda i: (i, 0))
    kern = functools.partial(_out_kernel, ff_chunk=ff_chunk)
    return pl.pallas_call(
        kern,
        grid=(nt,),
        in_specs=[row(d), row(GROUP_W), row(GROUP_W), mod_spec, mod_spec, mod_spec, mod_spec,
                  pl.BlockSpec((1, d), const), pl.BlockSpec((1, d), const),
                  pl.BlockSpec(w_out.shape, const), pl.BlockSpec(w1.shape, const),
                  pl.BlockSpec(w2.shape, const)],
        out_specs=row(d),
        out_shape=jax.ShapeDtypeStruct((t, d), F32),
        compiler_params=pltpu.CompilerParams(dimension_semantics=("arbitrary",),
                                             vmem_limit_bytes=VMEM_LIMIT),
        name="out_mlp",
    )(x, o_d, o_a, g1, sh2, sc2, g2, n2, nf, w_out, w1, w2)


def _split_w_in(w_in):
    ab0 = CONV_CH + GROUP_W
    w_main = jnp.concatenate([w_in[:, :ab0], w_in[:, ab0 + 2 * N_HEADS:]], axis=1).astype(BF16)
    w_ab = jnp.pad(w_in[:, ab0:ab0 + 2 * N_HEADS], ((0, 0), (0, LANES - 2 * N_HEADS)))
    return w_main, w_ab


def _lane_pad(v):
    return jnp.pad(v.astype(F32), (0, LANES - v.shape[0])).reshape(1, LANES)


def kernel(x_prompt, x_sample, cache_k, cache_v, state_delta, state_conv, page_table, c_prompt, c_sample,
           norm1_w, norm2_w, normf_w, w_ada, b_ada, w_in, w_out, conv_w, a_log, dt_bias, onorm_w,
           rel_bias, w_mlp1, w_mlp2):
    bp, tp, d = x_prompt.shape
    bs, ts, _ = x_sample.shape
    depth = w_in.shape[0]
    assert depth == 1 and d == 2 * GROUP_W
    l = 0
    n_pool, page = cache_k.shape[1], cache_k.shape[2]

    mod = _modulation(jnp.concatenate([c_prompt, c_sample], axis=0), w_ada[l], b_ada[l])
    mod = mod.reshape(bp + bs, 6, d)
    mod_p = [mod[:bp, i].reshape(bp, 1, d) for i in range(6)]
    tm_s = bs * ts
    mod_s = [jnp.broadcast_to(mod[bp:, i][:, None, :], (bs, ts, d)).reshape(1, tm_s, d) for i in range(6)]

    w_main, w_ab = _split_w_in(w_in[l])
    alog, dtb = _lane_pad(a_log[l]), _lane_pad(dt_bias[l])
    n1 = norm1_w[l].reshape(1, d)
    n2 = norm2_w[l].reshape(1, d)
    nf = normf_w.reshape(1, d)
    wo = w_out[l].astype(BF16)
    w1 = w_mlp1[l].astype(BF16)
    w2 = w_mlp2[l].astype(BF16)
    cw = conv_w[l]
    onw = onorm_w[l].reshape(1, HEAD_DIM)

    tm = 512
    xp = x_prompt.reshape(bp * tp, d)
    qkv_p, z_p, qa_p, ka_p, va_p, gb_p, gbt_p = _inproj(
        xp, mod_p[0], mod_p[1], n1, w_main, w_ab, alog, dtb, tm, tp)
    n_chunks = 8
    od_p, s_p = _gdn(qkv_p.reshape(bp, tp, CONV_CH), gb_p.reshape(bp, tp, LANES),
                     gbt_p.reshape(8, bp, tp).transpose(1, 0, 2), z_p.reshape(bp, tp, GROUP_W),
                     jnp.zeros((bp, CONV_W - 1, CONV_CH), F32),
                     jnp.zeros((bp, N_HEADS, HEAD_DIM, HEAD_DIM), F32), cw, onw, n_chunks, 1)
    oa_p = _moba_prompt(qa_p.reshape(bp, tp, GROUP_W), ka_p.reshape(bp, tp, GROUP_W),
                        va_p.reshape(bp, tp, GROUP_W), rel_bias)
    y_p = _out_mlp(xp, od_p.reshape(bp * tp, GROUP_W), oa_p.reshape(bp * tp, GROUP_W),
                   mod_p[2], mod_p[3], mod_p[4], mod_p[5], n2, nf, wo, w1, w2, tm, tp)

    xs = x_sample.reshape(tm_s, d)
    qkv_s, z_s, qa_s, ka_s, va_s, gb_s, gbt_s = _inproj(
        xs, mod_s[0], mod_s[1], n1, w_main, w_ab, alog, dtb, tm_s, tm_s)
    gb_s3 = jnp.pad(gb_s.reshape(bs, ts, LANES), ((0, 0), (0, CHUNK - ts), (0, 0)))
    gbt_s3 = jnp.pad(gbt_s.reshape(8, bs, ts).transpose(1, 0, 2), ((0, 0), (0, 0), (0, CHUNK - ts)))
    od_s, s_s = _gdn(qkv_s.reshape(bs, ts, CONV_CH), gb_s3, gbt_s3, z_s.reshape(bs, ts, GROUP_W),
                     state_conv[l], state_delta[l], cw, onw, 1, next(n for n in (4, 2, 1) if bs % n == 0))
    oa_s = _moba_decode(qa_s.reshape(bs, ts, GROUP_W), ka_s.reshape(bs, ts, GROUP_W),
                        va_s.reshape(bs, ts, GROUP_W),
                        cache_k.reshape(depth * n_pool, page * N_HEADS, HEAD_DIM),
                        cache_v.reshape(depth * n_pool, page * N_HEADS, HEAD_DIM),
                        page_table, rel_bias)
    y_s = _out_mlp(xs, od_s.reshape(tm_s, GROUP_W), oa_s.reshape(tm_s, GROUP_W),
                   mod_s[2], mod_s[3], mod_s[4], mod_s[5], n2, nf, wo, w1, w2, tm_s, tm_s)

    shp_p = (1, bp, tp, N_HEADS, HEAD_DIM)
    shp_s = (1, bs, ts, N_HEADS, HEAD_DIM)
    qkv_p3 = qkv_p.reshape(bp, tp, CONV_CH)
    qkv_s3 = qkv_s.reshape(bs, ts, CONV_CH)
    conv_s = jnp.concatenate([state_conv[l], qkv_s3], axis=1)[:, ts:]
    return (y_p.reshape(bp, tp, d), y_s.reshape(bs, ts, d),
            ka_p.reshape(shp_p), va_p.reshape(shp_p), s_p[None], qkv_p3[:, tp - (CONV_W - 1):][None],
            ka_s.reshape(shp_s), va_s.reshape(shp_s), s_s[None], conv_s[None])
```

```python
import functools
import math

import numpy as np
import jax
import jax.numpy as jnp
from jax import lax
from jax.experimental import pallas as pl
from jax.experimental.pallas import tpu as pltpu

F32 = jnp.float32
BF16 = jnp.bfloat16
HI = lax.Precision.HIGHEST
NEG_INF = float("-inf")
LOG2E = math.log2(math.e)

EPS = 1e-6
N_HEADS = 4
HEAD_DIM = 128
GROUP_W = N_HEADS * HEAD_DIM
CONV_W = 4
CONV_CH = 3 * GROUP_W
CHUNK = 64
STACK = N_HEADS * CHUNK
MOBA_BLOCK = 256
MOBA_TOPK = 3
NUM_BUCKETS = 32
MAX_DISTANCE = 128
LANES = 128
VMEM_LIMIT = 56 * 1024 * 1024


def _bucket_np(dist):
    n = np.maximum(dist, 0)
    max_exact = NUM_BUCKETS // 2
    nf = np.maximum(n, 1).astype(np.float32)
    large = max_exact + (np.log(nf / max_exact) / math.log(MAX_DISTANCE / max_exact)
                         * (NUM_BUCKETS - max_exact)).astype(np.int32)
    large = np.minimum(large, NUM_BUCKETS - 1)
    return np.where(n < max_exact, n, large).astype(np.int32)


def _mm(a, b):
    return jnp.dot(a.astype(BF16), b.astype(BF16), preferred_element_type=F32)


def _mm_nt(a, b):
    return lax.dot_general(a.astype(BF16), b.astype(BF16), (((1,), (1,)), ((), ())),
                           preferred_element_type=F32)


def _mm_tn(a, b):
    return lax.dot_general(a.astype(BF16), b.astype(BF16), (((0,), (0,)), ((), ())),
                           preferred_element_type=F32)


def _silu(x):
    return x * jax.nn.sigmoid(x)


def _softplus(x):
    return jnp.maximum(x, 0.0) + jnp.log1p(jnp.exp(-jnp.abs(x)))


def _mod_kernel(c_ref, w_ref, b_ref, o_ref):
    o_ref[...] = jnp.dot(_silu(c_ref[...]), w_ref[...], precision=HI,
                         preferred_element_type=F32) + b_ref[...]


def _modulation(c, w, b, tn=1536):
    m, d = c.shape
    n = w.shape[1]
    return pl.pallas_call(
        _mod_kernel,
        grid=(n // tn,),
        in_specs=[pl.BlockSpec((m, d), lambda j: (0, 0)),
                  pl.BlockSpec((d, tn), lambda j: (0, j)),
                  pl.BlockSpec((1, tn), lambda j: (0, j))],
        out_specs=pl.BlockSpec((m, tn), lambda j: (0, j)),
        out_shape=jax.ShapeDtypeStruct((m, n), F32),
        compiler_params=pltpu.CompilerParams(dimension_semantics=("arbitrary",),
                                             vmem_limit_bytes=VMEM_LIMIT),
        name="modulation",
    )(c, w, b.reshape(1, n))


def _inproj_kernel(x_ref, sh_ref, sc_ref, n1_ref, wm_ref, wab_ref, alog_ref, dtb_ref,
                   qkv_ref, z_ref, qa_ref, ka_ref, va_ref, gb_ref, gbt_ref):
    x = x_ref[...]
    r = lax.rsqrt(jnp.mean(x * x, axis=-1, keepdims=True) + EPS)
    h = (x * r * n1_ref[...]) * (1.0 + sc_ref[...]) + sh_ref[...]
    hb = h.astype(BF16)
    off = 0
    for ref in (qkv_ref, z_ref, qa_ref, ka_ref, va_ref):
        w = ref.shape[-1]
        ref[...] = jnp.dot(hb, wm_ref[:, off:off + w], preferred_element_type=F32)
        off += w
    ab = jnp.dot(h, wab_ref[...], precision=HI, preferred_element_type=F32)
    lane = lax.broadcasted_iota(jnp.int32, ab.shape, 1)
    g = -jnp.exp(alog_ref[...]) * _softplus(ab + dtb_ref[...])
    gb = jnp.where(lane < N_HEADS, g, jax.nn.sigmoid(ab))
    gb_ref[...] = gb
    gbt_ref[...] = gb.T[0:8, :]


def _inproj(x, shift, scale, n1, w_main, w_ab, alog, dtb, tm, rows_per_mod):
    t, d = x.shape
    nt = t // tm
    if shift.shape[1] == 1:
        per = rows_per_mod // tm
        mod_spec = pl.BlockSpec((None, 1, d), lambda i: (i // per, 0, 0))
    else:
        mod_spec = pl.BlockSpec((None, tm, d), lambda i: (i, 0, 0))
    const = lambda i: (0, 0)
    widths = (CONV_CH, GROUP_W, GROUP_W, GROUP_W, GROUP_W)
    outs = [jax.ShapeDtypeStruct((t, w), F32) for w in widths]
    outs += [jax.ShapeDtypeStruct((t, LANES), F32), jax.ShapeDtypeStruct((8, t), F32)]
    out_specs = [pl.BlockSpec((tm, w), lambda i: (i, 0)) for w in widths]
    out_specs += [pl.BlockSpec((tm, LANES), lambda i: (i, 0)), pl.BlockSpec((8, tm), lambda i: (0, i))]
    return pl.pallas_call(
        _inproj_kernel,
        grid=(nt,),
        in_specs=[pl.BlockSpec((tm, d), lambda i: (i, 0)), mod_spec, mod_spec,
                  pl.BlockSpec((1, d), const),
                  pl.BlockSpec(w_main.shape, const),
                  pl.BlockSpec(w_ab.shape, const),
                  pl.BlockSpec((1, LANES), const), pl.BlockSpec((1, LANES), const)],
        out_specs=out_specs,
        out_shape=outs,
        compiler_params=pltpu.CompilerParams(dimension_semantics=("arbitrary",),
                                             vmem_limit_bytes=VMEM_LIMIT),
        name="inproj",
    )(x, shift, scale, n1, w_main, w_ab, alog, dtb)


def _gdn_kernel(x_ref, gb_ref, gbt_ref, z_ref, cs_ref, s0_ref, cw_ref, onw_ref,
                o_ref, sn_ref, xbuf, s_scr, *, t_real, n_chunks, n_seq):
    ti = pl.program_id(1)
    t_tile = n_chunks * CHUNK

    @pl.when(ti == 0)
    def _():
        for n in range(n_seq):
            xbuf[n, 0:8, :] = jnp.zeros((8, CONV_CH), F32)
            xbuf[n, 5:8, :] = cs_ref[n]
            if t_real < t_tile:
                xbuf[n, 8:8 + t_tile, :] = jnp.zeros((t_tile, CONV_CH), F32)
        s_scr[...] = s0_ref[...]

    for n in range(n_seq):
        xbuf[n, 8:8 + t_real, :] = x_ref[n]

    ri = lax.broadcasted_iota(jnp.int32, (STACK, STACK), 0)
    ci = lax.broadcasted_iota(jnp.int32, (STACK, STACK), 1)
    same = (ri >> 6) == (ci >> 6)
    causal = same & ((ri & 63) >= (ci & 63))
    strict = same & ((ri & 63) > (ci & 63))
    eye = (ri == ci).astype(F32)
    l_st = (lax.broadcasted_iota(jnp.int32, (STACK, CHUNK), 1)
            <= (lax.broadcasted_iota(jnp.int32, (STACK, CHUNK), 0) & 63)).astype(F32)
    u_st = (lax.broadcasted_iota(jnp.int32, (CHUNK, STACK), 0)
            <= (lax.broadcasted_iota(jnp.int32, (CHUNK, STACK), 1) & 63)).astype(F32)
    row_head = lax.broadcasted_iota(jnp.int32, (STACK, LANES), 0) >> 6
    lane_id = lax.broadcasted_iota(jnp.int32, (STACK, LANES), 1)
    pick_g = lane_id == row_head
    pick_b = lane_id == row_head + N_HEADS
    col_head8 = lax.broadcasted_iota(jnp.int32, (8, STACK), 1) >> 6
    sub8 = lax.broadcasted_iota(jnp.int32, (8, STACK), 0)
    cw = cw_ref[...]
    onw = onw_ref[...]

    def stack(a):
        return jnp.concatenate([a[:, h * HEAD_DIM:(h + 1) * HEAD_DIM] for h in range(N_HEADS)], axis=0)

    def l2n(a):
        return a * lax.rsqrt(jnp.sum(a * a, axis=-1, keepdims=True) + EPS)

    items = [(n, c) for n in range(n_seq) for c in range(n_chunks)]

    def conv_qkv(n, c):
        base = 8 + c * CHUNK
        y = xbuf[n, base - 3:base - 3 + CHUNK, :] * cw[0:1, :]
        for i in range(1, CONV_W):
            y = y + xbuf[n, base - 3 + i:base - 3 + i + CHUNK, :] * cw[i:i + 1, :]
        y = _silu(y)
        q = l2n(stack(y[:, 0:GROUP_W])) * (HEAD_DIM ** -0.5)
        k = l2n(stack(y[:, GROUP_W:2 * GROUP_W]))
        v = stack(y[:, 2 * GROUP_W:3 * GROUP_W])
        return q, k, v

    def gates(n, c):
        gb = gb_ref[n, c * CHUNK:(c + 1) * CHUNK, :]
        gbt = gbt_ref[n, :, c * CHUNK:(c + 1) * CHUNK]
        gcum = jnp.dot(l_st, gb, precision=HI, preferred_element_type=F32)
        g_col = jnp.sum(jnp.where(pick_g, gcum, 0.0), axis=1, keepdims=True)
        g_last = gcum[CHUNK - 1:CHUNK, :]
        g_tot = jnp.sum(jnp.where(pick_g, jnp.broadcast_to(g_last, (STACK, LANES)), 0.0),
                        axis=1, keepdims=True)
        gb_st = jnp.concatenate([gb] * N_HEADS, axis=0)
        beta = jnp.sum(jnp.where(pick_b, gb_st, 0.0), axis=1, keepdims=True)
        grow_all = jnp.dot(gbt, u_st, precision=HI, preferred_element_type=F32)
        g_row = jnp.sum(jnp.where(sub8 == col_head8, grow_all, 0.0), axis=0, keepdims=True)
        decay = jnp.exp(jnp.where(causal, g_col - g_row, NEG_INF))
        return g_col, g_tot, g_last, beta, decay

    qkv = [conv_qkv(n, c) for n, c in items]
    gts = [gates(n, c) for n, c in items]
    kbs = [k.astype(BF16) for _, k, _ in qkv]
    a_mats = [jnp.where(strict, g[3] * _mm_nt(kb, kb) * g[4], 0.0) for kb, g in zip(kbs, gts)]
    invs = [eye - a for a in a_mats]
    pws = a_mats
    for _ in range(5):
        pws = [_mm(p, p) for p in pws]
        invs = [x + _mm(x, p) for x, p in zip(invs, pws)]
    resids = [eye - jnp.dot(eye + a, x, precision=HI, preferred_element_type=F32) for a, x in zip(a_mats, invs)]
    invs = [x + _mm(x, r) for x, r in zip(invs, resids)]
    exp_gs = [jnp.exp(g[0]) for g in gts]
    sols = [_mm(x, jnp.concatenate([v * g[3], k * (g[3] * eg)], axis=1))
            for x, (_, k, v), g, eg in zip(invs, qkv, gts, exp_gs)]
    qks = [_mm_nt(q, kb) * g[4] for (q, _, _), kb, g in zip(qkv, kbs, gts)]
    q_decs = [q * eg for (q, _, _), eg in zip(qkv, exp_gs)]
    k_tails = [k * jnp.exp(g[1] - g[0]) for (_, k, _), g in zip(qkv, gts)]

    for idx, (n, c) in enumerate(items):
        u0 = sols[idx][:, 0:HEAD_DIM]
        w = sols[idx][:, HEAD_DIM:2 * HEAD_DIM]
        us, qs = [], []
        for h in range(N_HEADS):
            sl = slice(h * CHUNK, (h + 1) * CHUNK)
            wq = jnp.concatenate([w[sl], q_decs[idx][sl]], axis=0)
            r = _mm(wq, s_scr[n, h])
            us.append(u0[sl] - r[0:CHUNK])
            qs.append(r[CHUNK:2 * CHUNK])
        u = jnp.concatenate(us, axis=0)
        o = jnp.concatenate(qs, axis=0) + _mm(qks[idx], u)
        for h in range(N_HEADS):
            sl = slice(h * CHUNK, (h + 1) * CHUNK)
            gl = jnp.exp(gts[idx][2][:, h:h + 1])
            s_scr[n, h] = s_scr[n, h] * gl + _mm_tn(k_tails[idx][sl], u[sl])

        o = o * lax.rsqrt(jnp.mean(o * o, axis=-1, keepdims=True) + EPS) * onw
        rows = min(CHUNK, t_real - c * CHUNK)
        for h in range(N_HEADS):
            zg = _silu(z_ref[n, c * CHUNK:c * CHUNK + rows, h * HEAD_DIM:(h + 1) * HEAD_DIM])
            o_ref[n, c * CHUNK:c * CHUNK + rows, h * HEAD_DIM:(h + 1) * HEAD_DIM] = \
                o[h * CHUNK:h * CHUNK + rows] * zg

    if t_real == t_tile:
        for n in range(n_seq):
            xbuf[n, 5:8, :] = xbuf[n, 8 + t_tile - 3:8 + t_tile, :]

    @pl.when(ti == pl.num_programs(1) - 1)
    def _():
        sn_ref[...] = s_scr[...]


def _gdn(qkv, gb, gbt, z, conv_state, s0, conv_w, onorm_w, n_chunks, n_seq):
    b, t, _ = qkv.shape
    t_tile = n_chunks * CHUNK
    t_real = min(t, t_tile)
    nt = max(1, t // t_tile)
    assert b % n_seq == 0
    kern = functools.partial(_gdn_kernel, t_real=t_real, n_chunks=n_chunks, n_seq=n_seq)
    return pl.pallas_call(
        kern,
        grid=(b // n_seq, nt),
        in_specs=[pl.BlockSpec((n_seq, t_real, CONV_CH), lambda i, j: (i, j, 0)),
                  pl.BlockSpec((n_seq, t_tile, LANES), lambda i, j: (i, j, 0)),
                  pl.BlockSpec((n_seq, 8, t_tile), lambda i, j: (i, 0, j)),
                  pl.BlockSpec((n_seq, t_real, GROUP_W), lambda i, j: (i, j, 0)),
                  pl.BlockSpec((n_seq, CONV_W - 1, CONV_CH), lambda i, j: (i, 0, 0)),
                  pl.BlockSpec((n_seq, N_HEADS, HEAD_DIM, HEAD_DIM), lambda i, j: (i, 0, 0, 0)),
                  pl.BlockSpec((CONV_W, CONV_CH), lambda i, j: (0, 0)),
                  pl.BlockSpec((1, HEAD_DIM), lambda i, j: (0, 0))],
        out_specs=[pl.BlockSpec((n_seq, t_real, GROUP_W), lambda i, j: (i, j, 0)),
                   pl.BlockSpec((n_seq, N_HEADS, HEAD_DIM, HEAD_DIM), lambda i, j: (i, 0, 0, 0))],
        out_shape=[jax.ShapeDtypeStruct((b, t, GROUP_W), F32),
                   jax.ShapeDtypeStruct((b, N_HEADS, HEAD_DIM, HEAD_DIM), F32)],
        scratch_shapes=[pltpu.VMEM((n_seq, 8 + t_tile, CONV_CH), F32),
                        pltpu.VMEM((n_seq, N_HEADS, HEAD_DIM, HEAD_DIM), F32)],
        compiler_params=pltpu.CompilerParams(dimension_semantics=("arbitrary", "arbitrary"),
                                             vmem_limit_bytes=VMEM_LIMIT),
        name="gated_delta",
    )(qkv, gb, gbt, z, conv_state, s0, conv_w, onorm_w)


def _bias_from_buckets(bkt, relb_ref, h):
    def body(t, b):
        return jnp.where(bkt == t, relb_ref[h, t], b)
    return lax.fori_loop(0, NUM_BUCKETS, body, jnp.zeros(bkt.shape, F32))


def _moba_prompt_kernel(relb_ref, bkt_ref, q_ref, qall_ref, k_ref, v_ref, o_ref,
                        kb_scr, vb_scr, kmean_scr, bias_scr, sel_scr, s_buf, p_buf, *, n_blocks, group, tiles):
    h = pl.program_id(1)
    step = pl.program_id(2)
    blk = MOBA_BLOCK
    scale = HEAD_DIM ** -0.5

    @pl.when(step == 0)
    def _():
        kb_scr[...] = k_ref[...].astype(BF16)
        vb_scr[:, 0:HEAD_DIM] = v_ref[...].astype(BF16)
        ones_col = lax.broadcasted_iota(jnp.int32, (k_ref.shape[0], HEAD_DIM), 1) == 0
        vb_scr[:, HEAD_DIM:2 * HEAD_DIM] = jnp.where(ones_col, 1.0, 0.0).astype(BF16)
        for j in range(n_blocks):
            kmean_scr[j:j + 1, :] = jnp.mean(k_ref[j * blk:(j + 1) * blk, :], axis=0, keepdims=True)
        bias_scr[...] = _bias_from_buckets(bkt_ref[...], relb_ref, h) * LOG2E
        t_all = qall_ref.shape[0]
        gate = lax.dot_general(kmean_scr[...], qall_ref[...], (((1,), (1,)), ((), ())), precision=HI,
                               preferred_element_type=F32)
        brow = lax.broadcasted_iota(jnp.int32, gate.shape, 0)
        own = lax.broadcasted_iota(jnp.int32, gate.shape, 1) >> (blk.bit_length() - 1)
        g = jnp.where(brow < own, gate, NEG_INF)
        sel_t = jnp.zeros(gate.shape, F32)
        for s in range(MOBA_TOPK):
            mx = jnp.max(g, axis=0, keepdims=True)
            idx = jnp.min(jnp.where(g == mx, brow, n_blocks), axis=0, keepdims=True)
            hit = brow == idx
            sel_t = jnp.maximum(sel_t, jnp.where(hit & (own > s), 1.0, 0.0))
            g = jnp.where(hit, NEG_INF, g)
        sel_pad = jnp.concatenate([sel_t, jnp.zeros((LANES - n_blocks, t_all), F32)], axis=0)
        for j in range(n_blocks):
            sel_scr[j * blk:(j + 1) * blk, :] = sel_pad[:, j * blk:(j + 1) * blk].T

    tq = range(tiles)
    qis = [step * tiles + t for t in tq]
    qb_all = q_ref[...].astype(BF16)
    qbs = [q_ref[t * blk:(t + 1) * blk, :].astype(BF16) for t in tq]
    far_bias = relb_ref[h, NUM_BUCKETS - 1]
    sels = [sel_scr[pl.ds(pl.multiple_of(qis[t] * blk, blk), blk), :] for t in tq]
    lane = lax.broadcasted_iota(jnp.int32, (blk, LANES), 1)
    prev_chosen = [jnp.sum(jnp.where(lane == qis[t] - 1, sels[t], 0.0), axis=1, keepdims=True) > 0.0 for t in tq]
    sel_far = [jnp.where(lane == qis[t] - 1, 0.0, sels[t]) for t in tq]

    ri = lax.broadcasted_iota(jnp.int32, (blk, blk), 0)
    ci = lax.broadcasted_iota(jnp.int32, (blk, blk), 1)
    own0 = [pl.multiple_of(qis[t] * blk, blk) for t in tq]
    prev0 = [pl.multiple_of(jnp.maximum(qis[t] - 1, 0) * blk, blk) for t in tq]
    kds = [jnp.concatenate([kb_scr[pl.ds(prev0[t], blk), :], kb_scr[pl.ds(own0[t], blk), :]], axis=0) for t in tq]
    vds = [jnp.concatenate([vb_scr[pl.ds(prev0[t], blk), :], vb_scr[pl.ds(own0[t], blk), :]], axis=0) for t in tq]
    t0s = [_mm_nt(qbs[t], kds[t]) * (scale * LOG2E) + bias_scr[...] for t in tq]
    keeps = [jnp.concatenate([jnp.broadcast_to(prev_chosen[t], (blk, blk)), ri >= ci], axis=1) for t in tq]
    t0s = [jnp.where(keeps[t], t0s[t], NEG_INF) for t in tq]
    m0s = [jnp.max(t0s[t], axis=1, keepdims=True) for t in tq]
    acc0s = [_mm(jnp.exp2(t0s[t] - m0s[t]), vds[t]) for t in tq]

    gw = group * blk
    n_groups = n_blocks // group
    far_t = far_bias * LOG2E

    def scores(g, slot):
        start = pl.multiple_of(g * gw, gw)
        s = _mm_nt(qb_all, kb_scr[pl.ds(start, gw), :])
        for t in tq:
            s_buf[t, slot] = s[t * blk:(t + 1) * blk]

    def flush(accs, alphas, g, slot):
        start = pl.multiple_of(g * gw, gw)
        vg = vb_scr[pl.ds(start, gw), :]
        return [alphas[t] * accs[t] + jnp.dot(p_buf[t, slot], vg, preferred_element_type=F32) for t in tq]

    def softmax_step(ms, g, slot):
        new_ms, alphas = [], []
        for t in tq:
            s = s_buf[t, slot]
            parts = []
            for u in range(group):
                chosen = jnp.sum(jnp.where(lane == g * group + u, sel_far[t], 0.0), axis=1, keepdims=True) > 0.0
                parts.append(s[:, u * blk:(u + 1) * blk] * (scale * LOG2E) + jnp.where(chosen, far_t, NEG_INF))
            tt = jnp.concatenate(parts, axis=1)
            m_new = jnp.maximum(ms[t], jnp.max(tt, axis=1, keepdims=True))
            p_buf[t, slot] = jnp.exp2(tt - m_new).astype(BF16)
            new_ms.append(m_new)
            alphas.append(jnp.exp2(ms[t] - m_new))
        return new_ms, alphas

    scores(0, 0)
    for t in tq:
        p_buf[t, 1] = jnp.zeros((blk, gw), BF16)
    last = n_groups - 1

    def body(i, carry):
        ms, accs, alpha_prev, g_prev = carry
        ms, accs, alpha_prev = list(ms), list(accs), list(alpha_prev)
        g_a = jnp.minimum(2 * i, last)
        g_b = jnp.minimum(2 * i + 1, last)
        scores(g_b, 1)
        accs = flush(accs, alpha_prev, g_prev, 1)
        ms, alpha_a = softmax_step(ms, 2 * i, 0)
        scores(jnp.minimum(2 * i + 2, last), 0)
        accs = flush(accs, alpha_a, g_a, 0)
        ms, alpha_b = softmax_step(ms, 2 * i + 1, 1)
        return tuple(ms), tuple(accs), tuple(alpha_b), g_b

    n_needed = (qis[-1] + group - 2) // group
    init = (tuple(m0s), tuple(acc0s), tuple(jnp.ones((blk, 1), F32) for _ in tq), 0)
    _, accs, alpha_prev, g_prev = lax.fori_loop(0, (n_needed + 1) // 2, body, init)
    accs = flush(list(accs), list(alpha_prev), g_prev, 1)
    for t in tq:
        o_ref[t * blk:(t + 1) * blk, :] = accs[t][:, 0:HEAD_DIM] / accs[t][:, HEAD_DIM:HEAD_DIM + 1]


def _moba_prompt(q, k, v, rel_bias, group=4, tiles=2):
    b, t, _ = q.shape
    blk = MOBA_BLOCK
    nb = t // blk
    r = np.arange(blk)[:, None]
    c = np.arange(blk)[None, :]
    bkt = jnp.asarray(np.concatenate([_bucket_np(blk + r - c), _bucket_np(r - c)], axis=1))
    assert nb % group == 0 and nb <= LANES and nb % tiles == 0
    kern = functools.partial(_moba_prompt_kernel, n_blocks=nb, group=group, tiles=tiles)
    qrows = tiles * blk
    whole = pl.BlockSpec((None, t, HEAD_DIM), lambda i, h, j: (i, 0, h))
    return pl.pallas_call(
        kern,
        grid=(b, N_HEADS, nb // tiles),
        in_specs=[pl.BlockSpec(memory_space=pltpu.SMEM),
                  pl.BlockSpec((blk, 2 * blk), lambda i, h, j: (0, 0)),
                  pl.BlockSpec((None, qrows, HEAD_DIM), lambda i, h, j: (i, j, h)),
                  whole, whole, whole],
        out_specs=pl.BlockSpec((None, qrows, HEAD_DIM), lambda i, h, j: (i, j, h)),
        out_shape=jax.ShapeDtypeStruct((b, t, GROUP_W), F32),
        scratch_shapes=[pltpu.VMEM((t, HEAD_DIM), BF16), pltpu.VMEM((t, 2 * HEAD_DIM), BF16),
                        pltpu.VMEM((nb, HEAD_DIM), F32), pltpu.VMEM((blk, 2 * blk), F32),
                        pltpu.VMEM((t, LANES), F32),
                        pltpu.VMEM((tiles, 2, blk, group * blk), F32),
                        pltpu.VMEM((tiles, 2, blk, group * blk), BF16)],
        compiler_params=pltpu.CompilerParams(dimension_semantics=("arbitrary", "arbitrary", "arbitrary"),
                                             vmem_limit_bytes=VMEM_LIMIT),
        name="moba_prompt",
    )(rel_bias, bkt, q, q, k, v)


def _moba_decode_kernel(pt_ref, relb_ref, bkt_ref, q_ref, kn_ref, vn_ref, *rest,
                        pages_per_step, n_blocks, n_new):
    del pt_ref
    npg = pages_per_step
    k_refs = rest[0:npg]
    v_refs = rest[npg:2 * npg]
    o_ref = rest[2 * npg]
    qbd_scr, m_scr, l_scr, acc_scr, kmean_scr, bias_scr = rest[2 * npg + 1:]
    jb = pl.program_id(1)
    blk = MOBA_BLOCK
    n_pairs = n_new * N_HEADS
    scale = HEAD_DIM ** -0.5
    bps = npg // 2

    lane1 = lax.broadcasted_iota(jnp.int32, (1, LANES), 1)
    far_row = jnp.zeros((1, LANES), F32)
    for h in range(N_HEADS):
        far_row = jnp.where((lane1 & 3) == h, relb_ref[h, NUM_BUCKETS - 1], far_row)

    @pl.when(jb == 0)
    def _():
        q = q_ref[...]
        rows = jnp.concatenate([jnp.broadcast_to(q[t:t + 1, :], (N_HEADS, GROUP_W)) for t in range(n_new)], axis=0)
        rp = lax.broadcasted_iota(jnp.int32, (n_pairs, GROUP_W), 0)
        cp = lax.broadcasted_iota(jnp.int32, (n_pairs, GROUP_W), 1)
        qbd = jnp.where((rp & 3) == (cp >> 7), rows, 0.0)
        qbd_scr[...] = jnp.zeros(qbd_scr.shape, F32)
        qbd_scr[0:n_pairs, :] = qbd
        bkt = bkt_ref[...]
        lane_h = lax.broadcasted_iota(jnp.int32, bkt.shape, 1) & 3
        b = jnp.zeros(bkt.shape, F32)
        for h in range(N_HEADS):
            b = jnp.where(lane_h == h, _bias_from_buckets(bkt, relb_ref, h), b)
        bias_scr[...] = b

    def load_page(ref):
        page = ref.shape[0] // N_HEADS
        return jnp.concatenate([ref[pl.ds(h, page, stride=N_HEADS), :] for h in range(N_HEADS)], axis=1)

    qbd = qbd_scr[...]
    j0 = pl.multiple_of(jb * bps, bps)
    k_all = jnp.concatenate([load_page(r) for r in k_refs], axis=0)
    kmean_scr[pl.ds(j0, bps), :] = jnp.mean(k_all.reshape(bps, blk, GROUP_W), axis=1)
    s = (_mm_nt(k_all, qbd) * scale).reshape(bps, blk, LANES)
    last_bias = jnp.where(jb == pl.num_programs(1) - 1, bias_scr[...], far_row)
    s = jnp.concatenate([s[0:bps - 1] + far_row, s[bps - 1:bps] + last_bias], axis=0)
    mj = jnp.max(s, axis=1, keepdims=True)
    p = jnp.exp(s - mj)
    m_scr[pl.ds(j0, bps), :] = mj.reshape(bps, LANES)
    l_scr[pl.ds(j0, bps), :] = jnp.sum(p, axis=1)
    p_t = p.reshape(bps * blk, LANES).T[0:n_pairs, :].astype(BF16)
    pvs = []
    for i in range(bps):
        vblk = jnp.concatenate([load_page(v_refs[2 * i]), load_page(v_refs[2 * i + 1])], axis=0)
        pvs.append(_mm(p_t[:, i * blk:(i + 1) * blk], vblk))
    acc_scr[pl.ds(pl.multiple_of(j0 * n_pairs, bps * n_pairs), bps * n_pairs), :] = jnp.concatenate(pvs, axis=0)

    @pl.when(jb == pl.num_programs(1) - 1)
    def _():
        gate = lax.dot_general(kmean_scr[...], qbd, (((1,), (1,)), ((), ())), precision=HI,
                               preferred_element_type=F32)
        row = lax.broadcasted_iota(jnp.int32, gate.shape, 0)
        g = gate
        sel = jnp.zeros(gate.shape, jnp.bool_)
        for _ in range(MOBA_TOPK):
            mx = jnp.max(g, axis=0, keepdims=True)
            idx = jnp.min(jnp.where(g == mx, row, n_blocks), axis=0, keepdims=True)
            hit = row == idx
            sel = sel | hit
            g = jnp.where(hit, NEG_INF, g)
        kn = jnp.concatenate([kn_ref[...], jnp.zeros((8 - n_new, GROUP_W), F32)], axis=0)
        s_own = _mm_nt(kn, qbd) * scale
        r8 = lax.broadcasted_iota(jnp.int32, (8, LANES), 0)
        l8 = lax.broadcasted_iota(jnp.int32, (8, LANES), 1)
        dist = (l8 >> 2) - r8
        own_bias = jnp.zeros((8, LANES), F32)
        for h in range(N_HEADS):
            for d in range(n_new):
                own_bias = jnp.where(((l8 & 3) == h) & (dist == d), relb_ref[h, d], own_bias)
        own_ok = (dist >= 0) & (r8 < n_new)
        s_own = jnp.where(own_ok, s_own + own_bias, NEG_INF)
        m_all = jnp.maximum(jnp.max(jnp.where(sel, m_scr[...], NEG_INF), axis=0, keepdims=True),
                            jnp.max(s_own, axis=0, keepdims=True))
        wgt = jnp.where(sel, jnp.exp(m_scr[...] - m_all), 0.0)
        p_own = jnp.exp(s_own - m_all)
        den = jnp.sum(wgt * l_scr[...], axis=0, keepdims=True) + jnp.sum(p_own, axis=0, keepdims=True)
        packed = jnp.concatenate([wgt, p_own, den, jnp.zeros((LANES - n_blocks - 9, LANES), F32)], axis=0)
        pk = packed.T
        out = jnp.zeros((n_pairs, GROUP_W), F32)
        for j in range(n_blocks):
            out = out + pk[0:n_pairs, j:j + 1] * acc_scr[j * n_pairs:(j + 1) * n_pairs, :]
        vn = vn_ref[...]
        for t in range(n_new):
            out = out + pk[0:n_pairs, n_blocks + t:n_blocks + t + 1] * vn[t:t + 1, :]
        out = out / pk[0:n_pairs, n_blocks + 8:n_blocks + 9]
        ph = lax.broadcasted_iota(jnp.int32, (n_pairs, HEAD_DIM), 0) & 3
        o16 = jnp.zeros((n_pairs, HEAD_DIM), F32)
        for h in range(N_HEADS):
            o16 = jnp.where(ph == h, out[:, h * HEAD_DIM:(h + 1) * HEAD_DIM], o16)
        for t in range(n_new):
            for h in range(N_HEADS):
                p = t * N_HEADS + h
                o_ref[t:t + 1, h * HEAD_DIM:(h + 1) * HEAD_DIM] = o16[p:p + 1, :]


def _moba_decode(q, k_new, v_new, cache_k, cache_v, page_table, rel_bias, pages_per_step=16):
    s, n_new, _ = q.shape
    n_pages = page_table.shape[1]
    page = cache_k.shape[1] // N_HEADS
    assert MOBA_BLOCK == 2 * page and n_pages % pages_per_step == 0 and pages_per_step % 16 == 0
    n_blocks = n_pages * page // MOBA_BLOCK
    assert n_blocks >= MOBA_TOPK and n_new <= 8
    npg = pages_per_step
    past = n_pages * page
    r = np.arange(MOBA_BLOCK)[:, None]
    lane_tok = (np.arange(LANES) >> 2)[None, :]
    bkt = jnp.asarray(_bucket_np(past + lane_tok - ((n_blocks - 1) * MOBA_BLOCK + r)))
    n_pairs = n_new * N_HEADS

    def page_spec(p):
        return pl.BlockSpec((None, page * N_HEADS, HEAD_DIM), lambda i, j, pt: (pt[i, j * npg + p], 0, 0))

    new_spec = pl.BlockSpec((None, n_new, GROUP_W), lambda i, j, pt: (i, 0, 0))
    kern = functools.partial(_moba_decode_kernel, pages_per_step=npg, n_blocks=n_blocks, n_new=n_new)
    grid_spec = pltpu.PrefetchScalarGridSpec(
        num_scalar_prefetch=1,
        grid=(s, n_pages // npg),
        in_specs=[pl.BlockSpec(memory_space=pltpu.SMEM),
                  pl.BlockSpec((MOBA_BLOCK, LANES), lambda i, j, pt: (0, 0)),
                  new_spec, new_spec, new_spec]
                 + [page_spec(p) for p in range(npg)] + [page_spec(p) for p in range(npg)],
        out_specs=new_spec,
        scratch_shapes=[pltpu.VMEM((LANES, GROUP_W), F32),
                        pltpu.VMEM((n_blocks, LANES), F32), pltpu.VMEM((n_blocks, LANES), F32),
                        pltpu.VMEM((n_blocks * n_pairs, GROUP_W), F32),
                        pltpu.VMEM((n_blocks, GROUP_W), F32),
                        pltpu.VMEM((MOBA_BLOCK, LANES), F32)],
    )
    return pl.pallas_call(
        kern,
        grid_spec=grid_spec,
        out_shape=jax.ShapeDtypeStruct((s, n_new, GROUP_W), F32),
        compiler_params=pltpu.CompilerParams(dimension_semantics=("arbitrary", "arbitrary"),
                                             vmem_limit_bytes=VMEM_LIMIT),
        name="moba_decode",
    )(page_table, rel_bias, bkt, q, k_new, v_new, *([cache_k] * npg), *([cache_v] * npg))


def _out_kernel(x_ref, od_ref, oa_ref, g1_ref, sh2_ref, sc2_ref, g2_ref, n2_ref, nf_ref,
                wo_ref, w1_ref, w2_ref, y_ref, *, ff_chunk):
    x = x_ref[...]
    mix = (jnp.dot(od_ref[...].astype(BF16), wo_ref[0:GROUP_W, :], preferred_element_type=F32)
           + jnp.dot(oa_ref[...].astype(BF16), wo_ref[GROUP_W:2 * GROUP_W, :], preferred_element_type=F32))
    x1 = x + g1_ref[...] * mix
    r = lax.rsqrt(jnp.mean(x1 * x1, axis=-1, keepdims=True) + EPS)
    h = ((x1 * r * n2_ref[...]) * (1.0 + sc2_ref[...]) + sh2_ref[...]).astype(BF16)
    d_ff = w1_ref.shape[1]
    ff = jnp.zeros(x.shape, F32)
    for c in range(d_ff // ff_chunk):
        a = jnp.dot(h, w1_ref[:, c * ff_chunk:(c + 1) * ff_chunk], preferred_element_type=F32)
        a = jnp.square(jnp.maximum(a, 0.0)).astype(BF16)
        ff = ff + jnp.dot(a, w2_ref[c * ff_chunk:(c + 1) * ff_chunk, :], preferred_element_type=F32)
    x2 = x1 + g2_ref[...] * ff
    r2 = lax.rsqrt(jnp.mean(x2 * x2, axis=-1, keepdims=True) + EPS)
    y_ref[...] = x2 * r2 * nf_ref[...]


def _out_mlp(x, o_d, o_a, g1, sh2, sc2, g2, n2, nf, w_out, w1, w2, tm, rows_per_mod, ff_chunk=1024):
    t, d = x.shape
    nt = t // tm
    if g1.shape[1] == 1:
        per = rows_per_mod // tm
        mod_spec = pl.BlockSpec((None, 1, d), lambda i: (i // per, 0, 0))
    else:
        mod_spec = pl.BlockSpec((None, tm, d), lambda i: (i, 0, 0))
    const = lambda i: (0, 0)
    row = lambda w: pl.BlockSpec((tm, w), lambda i: (i, 0))
    kern = functools.partial(_out_kernel, ff_chunk=ff_chunk)
    return pl.pallas_call(
        kern,
        grid=(nt,),
        in_specs=[row(d), row(GROUP_W), row(GROUP_W), mod_spec, mod_spec, mod_spec, mod_spec,
                  pl.BlockSpec((1, d), const), pl.BlockSpec((1, d), const),
                  pl.BlockSpec(w_out.shape, const), pl.BlockSpec(w1.shape, const),
                  pl.BlockSpec(w2.shape, const)],
        out_specs=row(d),
        out_shape=jax.ShapeDtypeStruct((t, d), F32),
        compiler_params=pltpu.CompilerParams(dimension_semantics=("arbitrary",),
                                             vmem_limit_bytes=VMEM_LIMIT),
        name="out_mlp",
    )(x, o_d, o_a, g1, sh2, sc2, g2, n2, nf, w_out, w1, w2)


def _split_w_in(w_in):
    ab0 = CONV_CH + GROUP_W
    w_main = jnp.concatenate([w_in[:, :ab0], w_in[:, ab0 + 2 * N_HEADS:]], axis=1).astype(BF16)
    w_ab = jnp.pad(w_in[:, ab0:ab0 + 2 * N_HEADS], ((0, 0), (0, LANES - 2 * N_HEADS)))
    return w_main, w_ab


def _lane_pad(v):
    return jnp.pad(v.astype(F32), (0, LANES - v.shape[0])).reshape(1, LANES)


def kernel(x_prompt, x_sample, cache_k, cache_v, state_delta, state_conv, page_table, c_prompt, c_sample,
           norm1_w, norm2_w, normf_w, w_ada, b_ada, w_in, w_out, conv_w, a_log, dt_bias, onorm_w,
           rel_bias, w_mlp1, w_mlp2):
    bp, tp, d = x_prompt.shape
    bs, ts, _ = x_sample.shape
    depth = w_in.shape[0]
    assert depth == 1 and d == 2 * GROUP_W
    l = 0
    n_pool, page = cache_k.shape[1], cache_k.shape[2]

    mod = _modulation(jnp.concatenate([c_prompt, c_sample], axis=0), w_ada[l], b_ada[l])
    mod = mod.reshape(bp + bs, 6, d)
    mod_p = [mod[:bp, i].reshape(bp, 1, d) for i in range(6)]
    tm_s = bs * ts
    mod_s = [jnp.broadcast_to(mod[bp:, i][:, None, :], (bs, ts, d)).reshape(1, tm_s, d) for i in range(6)]

    w_main, w_ab = _split_w_in(w_in[l])
    alog, dtb = _lane_pad(a_log[l]), _lane_pad(dt_bias[l])
    n1 = norm1_w[l].reshape(1, d)
    n2 = norm2_w[l].reshape(1, d)
    nf = normf_w.reshape(1, d)
    wo = w_out[l].astype(BF16)
    w1 = w_mlp1[l].astype(BF16)
    w2 = w_mlp2[l].astype(BF16)
    cw = conv_w[l]
    onw = onorm_w[l].reshape(1, HEAD_DIM)

    tm = 512
    xp = x_prompt.reshape(bp * tp, d)
    qkv_p, z_p, qa_p, ka_p, va_p, gb_p, gbt_p = _inproj(
        xp, mod_p[0], mod_p[1], n1, w_main, w_ab, alog, dtb, tm, tp)
    n_chunks = 8
    od_p, s_p = _gdn(qkv_p.reshape(bp, tp, CONV_CH), gb_p.reshape(bp, tp, LANES),
                     gbt_p.reshape(8, bp, tp).transpose(1, 0, 2), z_p.reshape(bp, tp, GROUP_W),
                     jnp.zeros((bp, CONV_W - 1, CONV_CH), F32),
                     jnp.zeros((bp, N_HEADS, HEAD_DIM, HEAD_DIM), F32), cw, onw, n_chunks, 1)
    oa_p = _moba_prompt(qa_p.reshape(bp, tp, GROUP_W), ka_p.reshape(bp, tp, GROUP_W),
                        va_p.reshape(bp, tp, GROUP_W), rel_bias)
    y_p = _out_mlp(xp, od_p.reshape(bp * tp, GROUP_W), oa_p.reshape(bp * tp, GROUP_W),
                   mod_p[2], mod_p[3], mod_p[4], mod_p[5], n2, nf, wo, w1, w2, tm, tp)

    xs = x_sample.reshape(tm_s, d)
    qkv_s, z_s, qa_s, ka_s, va_s, gb_s, gbt_s = _inproj(
        xs, mod_s[0], mod_s[1], n1, w_main, w_ab, alog, dtb, tm_s, tm_s)
    gb_s3 = jnp.pad(gb_s.reshape(bs, ts, LANES), ((0, 0), (0, CHUNK - ts), (0, 0)))
    gbt_s3 = jnp.pad(gbt_s.reshape(8, bs, ts).transpose(1, 0, 2), ((0, 0), (0, 0), (0, CHUNK - ts)))
    od_s, s_s = _gdn(qkv_s.reshape(bs, ts, CONV_CH), gb_s3, gbt_s3, z_s.reshape(bs, ts, GROUP_W),
                     state_conv[l], state_delta[l], cw, onw, 1, next(n for n in (4, 2, 1) if bs % n == 0))
    oa_s = _moba_decode(qa_s.reshape(bs, ts, GROUP_W), ka_s.reshape(bs, ts, GROUP_W),
                        va_s.reshape(bs, ts, GROUP_W),
                        cache_k.reshape(depth * n_pool, page * N_HEADS, HEAD_DIM),
                        cache_v.reshape(depth * n_pool, page * N_HEADS, HEAD_DIM),
                        page_table, rel_bias)
    y_s = _out_mlp(xs, od_s.reshape(tm_s, GROUP_W), oa_s.reshape(tm_s, GROUP_W),
                   mod_s[2], mod_s[3], mod_s[4], mod_s[5], n2, nf, wo, w1, w2, tm_s, tm_s)

    shp_p = (1, bp, tp, N_HEADS, HEAD_DIM)
    shp_s = (1, bs, ts, N_HEADS, HEAD_DIM)
    qkv_p3 = qkv_p.reshape(bp, tp, CONV_CH)
    qkv_s3 = qkv_s.reshape(bs, ts, CONV_CH)
    conv_s = jnp.concatenate([state_conv[l], qkv_s3], axis=1)[:, ts:]
    return (y_p.reshape(bp, tp, d), y_s.reshape(bs, ts, d),
            ka_p.reshape(shp_p), va_p.reshape(shp_p), s_p[None], qkv_p3[:, tp - (CONV_W - 1):][None],
            ka_s.reshape(shp_s), va_s.reshape(shp_s), s_s[None], conv_s[None])
```

```python
import functools
import math

import numpy as np
import jax
import jax.numpy as jnp
from jax import lax
from jax.experimental import pallas as pl
from jax.experimental.pallas import tpu as pltpu

F32 = jnp.float32
BF16 = jnp.bfloat16
HI = lax.Precision.HIGHEST
NEG_INF = float("-inf")
LOG2E = math.log2(math.e)

EPS = 1e-6
N_HEADS = 4
HEAD_DIM = 128
GROUP_W = N_HEADS * HEAD_DIM
CONV_W = 4
CONV_CH = 3 * GROUP_W
CHUNK = 64
STACK = N_HEADS * CHUNK
MOBA_BLOCK = 256
MOBA_TOPK = 3
NUM_BUCKETS = 32
MAX_DISTANCE = 128
LANES = 128
VT_ROWS = HEAD_DIM + 16
VMEM_LIMIT = 56 * 1024 * 1024


def _bucket_np(dist):
    n = np.maximum(dist, 0)
    max_exact = NUM_BUCKETS // 2
    nf = np.maximum(n, 1).astype(np.float32)
    large = max_exact + (np.log(nf / max_exact) / math.log(MAX_DISTANCE / max_exact)
                         * (NUM_BUCKETS - max_exact)).astype(np.int32)
    large = np.minimum(large, NUM_BUCKETS - 1)
    return np.where(n < max_exact, n, large).astype(np.int32)


def _mm(a, b):
    return jnp.dot(a.astype(BF16), b.astype(BF16), preferred_element_type=F32)


def _mm_nt(a, b):
    return lax.dot_general(a.astype(BF16), b.astype(BF16), (((1,), (1,)), ((), ())),
                           preferred_element_type=F32)


def _mm_tn(a, b):
    return lax.dot_general(a.astype(BF16), b.astype(BF16), (((0,), (0,)), ((), ())),
                           preferred_element_type=F32)


def _silu(x):
    return x * jax.nn.sigmoid(x)


def _softplus(x):
    return jnp.maximum(x, 0.0) + jnp.log1p(jnp.exp(-jnp.abs(x)))


def _mod_kernel(c_ref, w_ref, b_ref, o_ref):
    o_ref[...] = jnp.dot(_silu(c_ref[...]), w_ref[...], precision=HI,
                         preferred_element_type=F32) + b_ref[...]


def _modulation(c, w, b, tn=1536):
    m, d = c.shape
    n = w.shape[1]
    return pl.pallas_call(
        _mod_kernel,
        grid=(n // tn,),
        in_specs=[pl.BlockSpec((m, d), lambda j: (0, 0)),
                  pl.BlockSpec((d, tn), lambda j: (0, j)),
                  pl.BlockSpec((1, tn), lambda j: (0, j))],
        out_specs=pl.BlockSpec((m, tn), lambda j: (0, j)),
        out_shape=jax.ShapeDtypeStruct((m, n), F32),
        compiler_params=pltpu.CompilerParams(dimension_semantics=("arbitrary",),
                                             vmem_limit_bytes=VMEM_LIMIT),
        name="modulation",
    )(c, w, b.reshape(1, n))


def _inproj_kernel(x_ref, sh_ref, sc_ref, n1_ref, wm_ref, wab_ref, alog_ref, dtb_ref,
                   qkv_ref, z_ref, qa_ref, ka_ref, va_ref, gb_ref, gbt_ref):
    x = x_ref[...]
    r = lax.rsqrt(jnp.mean(x * x, axis=-1, keepdims=True) + EPS)
    h = (x * r * n1_ref[...]) * (1.0 + sc_ref[...]) + sh_ref[...]
    hb = h.astype(BF16)
    off = 0
    for ref in (qkv_ref, z_ref, qa_ref, ka_ref, va_ref):
        w = ref.shape[-1]
        ref[...] = jnp.dot(hb, wm_ref[:, off:off + w], preferred_element_type=F32)
        off += w
    h_lo = (h - hb.astype(F32)).astype(BF16)
    ab2 = jnp.dot(hb, wab_ref[...], preferred_element_type=F32)
    ab = (ab2[:, 0:LANES] + ab2[:, LANES:2 * LANES]
          + jnp.dot(h_lo, wab_ref[:, 0:LANES], preferred_element_type=F32))
    lane = lax.broadcasted_iota(jnp.int32, ab.shape, 1)
    g = -jnp.exp(alog_ref[...]) * _softplus(ab + dtb_ref[...])
    gb = jnp.where(lane < N_HEADS, g, jax.nn.sigmoid(ab))
    gb_ref[...] = gb
    gbt_ref[...] = gb.T[0:8, :]


def _inproj(x, shift, scale, n1, w_main, w_ab, alog, dtb, tm, rows_per_mod):
    t, d = x.shape
    nt = t // tm
    if shift.shape[1] == 1:
        per = rows_per_mod // tm
        mod_spec = pl.BlockSpec((None, 1, d), lambda i: (i // per, 0, 0))
    else:
        mod_spec = pl.BlockSpec((None, tm, d), lambda i: (i, 0, 0))
    const = lambda i: (0, 0)
    widths = (CONV_CH, GROUP_W, GROUP_W, GROUP_W, GROUP_W)
    outs = [jax.ShapeDtypeStruct((t, w), F32) for w in widths]
    outs += [jax.ShapeDtypeStruct((t, LANES), F32), jax.ShapeDtypeStruct((8, t), F32)]
    out_specs = [pl.BlockSpec((tm, w), lambda i: (i, 0)) for w in widths]
    out_specs += [pl.BlockSpec((tm, LANES), lambda i: (i, 0)), pl.BlockSpec((8, tm), lambda i: (0, i))]
    in_specs = [pl.BlockSpec((tm, d), lambda i: (i, 0)), mod_spec, mod_spec,
                pl.BlockSpec((1, d), const),
                pl.BlockSpec(w_main.shape, const),
                pl.BlockSpec(w_ab.shape, const),
                pl.BlockSpec((1, LANES), const), pl.BlockSpec((1, LANES), const)]
    return pl.pallas_call(
        _inproj_kernel,
        grid=(nt,),
        in_specs=in_specs,
        out_specs=out_specs,
        out_shape=outs,
        compiler_params=pltpu.CompilerParams(dimension_semantics=("arbitrary",),
                                             vmem_limit_bytes=VMEM_LIMIT),
        name="inproj",
    )(x, shift, scale, n1, w_main, w_ab, alog, dtb)


def _gdn_kernel(x_ref, gb_ref, gbt_ref, z_ref, cs_ref, s0_ref, cw_ref, onw_ref,
                o_ref, sn_ref, xbuf, s_scr, *, t_real, n_chunks, n_seq):
    ti = pl.program_id(1)
    t_tile = n_chunks * CHUNK

    @pl.when(ti == 0)
    def _():
        for n in range(n_seq):
            xbuf[n, 0:8, :] = jnp.zeros((8, CONV_CH), F32)
            xbuf[n, 5:8, :] = cs_ref[n]
            if t_real < t_tile:
                xbuf[n, 8:8 + t_tile, :] = jnp.zeros((t_tile, CONV_CH), F32)
        s_scr[...] = s0_ref[...]

    for n in range(n_seq):
        xbuf[n, 8:8 + t_real, :] = x_ref[n]

    ri = lax.broadcasted_iota(jnp.int32, (STACK, STACK), 0)
    ci = lax.broadcasted_iota(jnp.int32, (STACK, STACK), 1)
    same = (ri >> 6) == (ci >> 6)
    causal = same & ((ri & 63) >= (ci & 63))
    strict = same & ((ri & 63) > (ci & 63))
    eye = (ri == ci).astype(F32)
    l_st = (lax.broadcasted_iota(jnp.int32, (STACK, CHUNK), 1)
            <= (lax.broadcasted_iota(jnp.int32, (STACK, CHUNK), 0) & 63)).astype(F32)
    u_st = (lax.broadcasted_iota(jnp.int32, (CHUNK, STACK), 0)
            <= (lax.broadcasted_iota(jnp.int32, (CHUNK, STACK), 1) & 63)).astype(F32)
    row_head = lax.broadcasted_iota(jnp.int32, (STACK, LANES), 0) >> 6
    lane_id = lax.broadcasted_iota(jnp.int32, (STACK, LANES), 1)
    pick_g = lane_id == row_head
    pick_b = lane_id == row_head + N_HEADS
    col_head8 = lax.broadcasted_iota(jnp.int32, (8, STACK), 1) >> 6
    sub8 = lax.broadcasted_iota(jnp.int32, (8, STACK), 0)
    cw = cw_ref[...]
    onw = onw_ref[...]

    def stack(a):
        return jnp.concatenate([a[:, h * HEAD_DIM:(h + 1) * HEAD_DIM] for h in range(N_HEADS)], axis=0)

    def l2n(a):
        return a * lax.rsqrt(jnp.sum(a * a, axis=-1, keepdims=True) + EPS)

    items = [(n, c) for n in range(n_seq) for c in range(n_chunks)]

    def conv_qkv(n, c):
        base = 8 + c * CHUNK
        y = xbuf[n, base - 3:base - 3 + CHUNK, :] * cw[0:1, :]
        for i in range(1, CONV_W):
            y = y + xbuf[n, base - 3 + i:base - 3 + i + CHUNK, :] * cw[i:i + 1, :]
        y = _silu(y)
        q = l2n(stack(y[:, 0:GROUP_W])) * (HEAD_DIM ** -0.5)
        k = l2n(stack(y[:, GROUP_W:2 * GROUP_W]))
        v = stack(y[:, 2 * GROUP_W:3 * GROUP_W])
        return q, k, v

    def gates(n, c):
        gb = gb_ref[n, c * CHUNK:(c + 1) * CHUNK, :]
        gbt = gbt_ref[n, :, c * CHUNK:(c + 1) * CHUNK]
        gcum = jnp.dot(l_st, gb, precision=HI, preferred_element_type=F32)
        g_col = jnp.sum(jnp.where(pick_g, gcum, 0.0), axis=1, keepdims=True)
        g_last = gcum[CHUNK - 1:CHUNK, :]
        g_tot = jnp.sum(jnp.where(pick_g, jnp.broadcast_to(g_last, (STACK, LANES)), 0.0),
                        axis=1, keepdims=True)
        gb_st = jnp.concatenate([gb] * N_HEADS, axis=0)
        beta = jnp.sum(jnp.where(pick_b, gb_st, 0.0), axis=1, keepdims=True)
        grow_all = jnp.dot(gbt, u_st, precision=HI, preferred_element_type=F32)
        g_row = jnp.sum(jnp.where(sub8 == col_head8, grow_all, 0.0), axis=0, keepdims=True)
        decay = jnp.exp(jnp.where(causal, g_col - g_row, NEG_INF))
        return g_col, g_tot, g_last, beta, decay

    qkv = [conv_qkv(n, c) for n, c in items]
    gts = [gates(n, c) for n, c in items]
    kbs = [k.astype(BF16) for _, k, _ in qkv]
    a_mats = [jnp.where(strict, g[3] * _mm_nt(kb, kb) * g[4], 0.0) for kb, g in zip(kbs, gts)]
    invs = [eye - a for a in a_mats]
    pws = a_mats
    for _ in range(5):
        pws = [_mm(p, p) for p in pws]
        invs = [x + _mm(x, p) for x, p in zip(invs, pws)]
    resids = [eye - jnp.dot(eye + a, x, precision=HI, preferred_element_type=F32) for a, x in zip(a_mats, invs)]
    invs = [x + _mm(x, r) for x, r in zip(invs, resids)]
    exp_gs = [jnp.exp(g[0]) for g in gts]
    sols = [_mm(x, jnp.concatenate([v * g[3], k * (g[3] * eg)], axis=1))
            for x, (_, k, v), g, eg in zip(invs, qkv, gts, exp_gs)]
    qks = [_mm_nt(q, kb) * g[4] for (q, _, _), kb, g in zip(qkv, kbs, gts)]
    q_decs = [q * eg for (q, _, _), eg in zip(qkv, exp_gs)]
    k_tails = [k * jnp.exp(g[1] - g[0]) for (_, k, _), g in zip(qkv, gts)]

    for idx, (n, c) in enumerate(items):
        u0 = sols[idx][:, 0:HEAD_DIM]
        w = sols[idx][:, HEAD_DIM:2 * HEAD_DIM]
        us, qs = [], []
        for h in range(N_HEADS):
            sl = slice(h * CHUNK, (h + 1) * CHUNK)
            wq = jnp.concatenate([w[sl], q_decs[idx][sl]], axis=0)
            r = _mm(wq, s_scr[n, h])
            us.append(u0[sl] - r[0:CHUNK])
            qs.append(r[CHUNK:2 * CHUNK])
        u = jnp.concatenate(us, axis=0)
        o = jnp.concatenate(qs, axis=0) + _mm(qks[idx], u)
        for h in range(N_HEADS):
            sl = slice(h * CHUNK, (h + 1) * CHUNK)
            gl = jnp.exp(gts[idx][2][:, h:h + 1])
            s_scr[n, h] = s_scr[n, h] * gl + _mm_tn(k_tails[idx][sl], u[sl])

        o = o * lax.rsqrt(jnp.mean(o * o, axis=-1, keepdims=True) + EPS) * onw
        rows = min(CHUNK, t_real - c * CHUNK)
        for h in range(N_HEADS):
            zg = _silu(z_ref[n, c * CHUNK:c * CHUNK + rows, h * HEAD_DIM:(h + 1) * HEAD_DIM])
            o_ref[n, c * CHUNK:c * CHUNK + rows, h * HEAD_DIM:(h + 1) * HEAD_DIM] = \
                o[h * CHUNK:h * CHUNK + rows] * zg

    if t_real == t_tile:
        for n in range(n_seq):
            xbuf[n, 5:8, :] = xbuf[n, 8 + t_tile - 3:8 + t_tile, :]

    @pl.when(ti == pl.num_programs(1) - 1)
    def _():
        sn_ref[...] = s_scr[...]


def _gdn(qkv, gb, gbt, z, conv_state, s0, conv_w, onorm_w, n_chunks, n_seq):
    b, t, _ = qkv.shape
    t_tile = n_chunks * CHUNK
    t_real = min(t, t_tile)
    nt = max(1, t // t_tile)
    assert b % n_seq == 0
    kern = functools.partial(_gdn_kernel, t_real=t_real, n_chunks=n_chunks, n_seq=n_seq)
    return pl.pallas_call(
        kern,
        grid=(b // n_seq, nt),
        in_specs=[pl.BlockSpec((n_seq, t_real, CONV_CH), lambda i, j: (i, j, 0)),
                  pl.BlockSpec((n_seq, t_tile, LANES), lambda i, j: (i, j, 0)),
                  pl.BlockSpec((n_seq, 8, t_tile), lambda i, j: (i, 0, j)),
                  pl.BlockSpec((n_seq, t_real, GROUP_W), lambda i, j: (i, j, 0)),
                  pl.BlockSpec((n_seq, CONV_W - 1, CONV_CH), lambda i, j: (i, 0, 0)),
                  pl.BlockSpec((n_seq, N_HEADS, HEAD_DIM, HEAD_DIM), lambda i, j: (i, 0, 0, 0)),
                  pl.BlockSpec((CONV_W, CONV_CH), lambda i, j: (0, 0)),
                  pl.BlockSpec((1, HEAD_DIM), lambda i, j: (0, 0))],
        out_specs=[pl.BlockSpec((n_seq, t_real, GROUP_W), lambda i, j: (i, j, 0)),
                   pl.BlockSpec((n_seq, N_HEADS, HEAD_DIM, HEAD_DIM), lambda i, j: (i, 0, 0, 0))],
        out_shape=[jax.ShapeDtypeStruct((b, t, GROUP_W), F32),
                   jax.ShapeDtypeStruct((b, N_HEADS, HEAD_DIM, HEAD_DIM), F32)],
        scratch_shapes=[pltpu.VMEM((n_seq, 8 + t_tile, CONV_CH), F32),
                        pltpu.VMEM((n_seq, N_HEADS, HEAD_DIM, HEAD_DIM), F32)],
        compiler_params=pltpu.CompilerParams(dimension_semantics=("arbitrary", "arbitrary"),
                                             vmem_limit_bytes=VMEM_LIMIT),
        name="gated_delta",
    )(qkv, gb, gbt, z, conv_state, s0, conv_w, onorm_w)


def _bias_from_buckets(bkt, relb_ref, h):
    def body(t, b):
        return jnp.where(bkt == t, relb_ref[h, t], b)
    return lax.fori_loop(0, NUM_BUCKETS, body, jnp.zeros(bkt.shape, F32))


def _moba_prompt_kernel(relb_ref, bkt_ref, q_ref, qall_ref, k_ref, v_ref, o_ref,
                        kb_scr, vt_scr, kmean_scr, bias_scr, sel_scr, s_buf, p_buf, *, n_blocks, group, tiles):
    h = pl.program_id(1)
    step = pl.program_id(2)
    blk = MOBA_BLOCK
    scale = HEAD_DIM ** -0.5

    @pl.when(step == 0)
    def _():
        kb_scr[...] = k_ref[...].astype(BF16)
        for j in range(n_blocks):
            vt_scr[0:HEAD_DIM, j * blk:(j + 1) * blk] = v_ref[j * blk:(j + 1) * blk, :].T.astype(BF16)
            kmean_scr[j:j + 1, :] = jnp.mean(k_ref[j * blk:(j + 1) * blk, :], axis=0, keepdims=True)
        tail = (VT_ROWS - HEAD_DIM, k_ref.shape[0])
        vt_scr[HEAD_DIM:VT_ROWS, :] = jnp.where(lax.broadcasted_iota(jnp.int32, tail, 0) == 0, 1.0, 0.0).astype(BF16)
        bias_scr[...] = _bias_from_buckets(bkt_ref[...], relb_ref, h) * LOG2E
        t_all = qall_ref.shape[0]
        gate = lax.dot_general(kmean_scr[...], qall_ref[...], (((1,), (1,)), ((), ())), precision=HI,
                               preferred_element_type=F32)
        brow = lax.broadcasted_iota(jnp.int32, gate.shape, 0)
        own = lax.broadcasted_iota(jnp.int32, gate.shape, 1) >> (blk.bit_length() - 1)
        g = jnp.where(brow < own, gate, NEG_INF)
        sel_t = jnp.zeros(gate.shape, F32)
        for s in range(MOBA_TOPK):
            mx = jnp.max(g, axis=0, keepdims=True)
            idx = jnp.min(jnp.where(g == mx, brow, n_blocks), axis=0, keepdims=True)
            hit = brow == idx
            sel_t = jnp.maximum(sel_t, jnp.where(hit & (own > s), 1.0, 0.0))
            g = jnp.where(hit, NEG_INF, g)
        sel_scr[...] = sel_t

    tq = range(tiles)
    qis = [step * tiles + t for t in tq]
    qb_all = q_ref[...].astype(BF16)
    qbs = [q_ref[t * blk:(t + 1) * blk, :].astype(BF16) for t in tq]
    far_bias = relb_ref[h, NUM_BUCKETS - 1]
    own0 = [pl.multiple_of(qis[t] * blk, blk) for t in tq]
    prev0 = [pl.multiple_of(jnp.maximum(qis[t] - 1, 0) * blk, blk) for t in tq]
    sels = [sel_scr[:, pl.ds(own0[t], blk)] for t in tq]
    brow = lax.broadcasted_iota(jnp.int32, (n_blocks, blk), 0)

    def block_row(sel, j):
        return jnp.sum(jnp.where(brow == j, sel, 0.0), axis=0, keepdims=True)

    prev_chosen = [block_row(sels[t], qis[t] - 1) > 0.0 for t in tq]
    sel_far = [jnp.where(brow == qis[t] - 1, 0.0, sels[t]) for t in tq]

    ki = lax.broadcasted_iota(jnp.int32, (blk, blk), 0)
    qcol = lax.broadcasted_iota(jnp.int32, (blk, blk), 1)
    kds = [jnp.concatenate([kb_scr[pl.ds(prev0[t], blk), :], kb_scr[pl.ds(own0[t], blk), :]], axis=0) for t in tq]
    vds = [jnp.concatenate([vt_scr[:, pl.ds(prev0[t], blk)], vt_scr[:, pl.ds(own0[t], blk)]], axis=1) for t in tq]
    t0s = [_mm_nt(kds[t], qbs[t]) * (scale * LOG2E) + bias_scr[...] for t in tq]
    keeps = [jnp.concatenate([jnp.broadcast_to(prev_chosen[t], (blk, blk)), ki <= qcol], axis=0) for t in tq]
    t0s = [jnp.where(keeps[t], t0s[t], NEG_INF) for t in tq]
    m0s = [jnp.max(t0s[t], axis=0, keepdims=True) for t in tq]
    acc0s = [jnp.dot(vds[t], jnp.exp2(t0s[t] - m0s[t]).astype(BF16), preferred_element_type=F32)
             for t in tq]

    gw = group * blk
    n_groups = n_blocks // group
    far_t = far_bias * LOG2E

    def scores(g, slot):
        start = pl.multiple_of(g * gw, gw)
        s = _mm_nt(kb_scr[pl.ds(start, gw), :], qb_all)
        for t in tq:
            s_buf[t, slot] = s[:, t * blk:(t + 1) * blk]

    def flush(accs, alphas, g, slot):
        start = pl.multiple_of(g * gw, gw)
        vg = vt_scr[:, pl.ds(start, gw)]
        return [alphas[t] * accs[t] + jnp.dot(vg, p_buf[t, slot], preferred_element_type=F32) for t in tq]

    def softmax_step(ms, g, slot):
        new_ms, alphas = [], []
        for t in tq:
            s = s_buf[t, slot]
            tt = jnp.concatenate(
                [s[u * blk:(u + 1) * blk] * (scale * LOG2E)
                 + jnp.where(block_row(sel_far[t], g * group + u) > 0.0, far_t, NEG_INF) for u in range(group)], axis=0)
            m_new = jnp.maximum(ms[t], jnp.max(tt, axis=0, keepdims=True))
            p_buf[t, slot] = jnp.exp2(tt - m_new).astype(BF16)
            new_ms.append(m_new)
            alphas.append(jnp.exp2(ms[t] - m_new))
        return new_ms, alphas

    scores(0, 0)
    for t in tq:
        p_buf[t, 1] = jnp.zeros((gw, blk), BF16)
    last = n_groups - 1

    def body(i, carry):
        ms, accs, alpha_prev, g_prev = carry
        ms, accs, alpha_prev = list(ms), list(accs), list(alpha_prev)
        g_a = jnp.minimum(2 * i, last)
        g_b = jnp.minimum(2 * i + 1, last)
        scores(g_b, 1)
        accs = flush(accs, alpha_prev, g_prev, 1)
        ms, alpha_a = softmax_step(ms, 2 * i, 0)
        scores(jnp.minimum(2 * i + 2, last), 0)
        accs = flush(accs, alpha_a, g_a, 0)
        ms, alpha_b = softmax_step(ms, 2 * i + 1, 1)
        return tuple(ms), tuple(accs), tuple(alpha_b), g_b

    n_needed = (qis[-1] + group - 2) // group
    init = (tuple(m0s), tuple(acc0s), tuple(jnp.ones((1, blk), F32) for _ in tq), 0)
    _, accs, alpha_prev, g_prev = lax.fori_loop(0, (n_needed + 1) // 2, body, init)
    accs = flush(list(accs), list(alpha_prev), g_prev, 1)
    for t in tq:
        o_t = accs[t][0:HEAD_DIM] / accs[t][HEAD_DIM:HEAD_DIM + 1]
        o_ref[t * blk:(t + 1) * blk, :] = o_t.T


def _moba_prompt(q, k, v, rel_bias, group=4, tiles=2):
    b, t, _ = q.shape
    blk = MOBA_BLOCK
    nb = t // blk
    kk = np.arange(blk)[:, None]
    qq = np.arange(blk)[None, :]
    bkt = jnp.asarray(np.concatenate([_bucket_np(blk + qq - kk), _bucket_np(qq - kk)], axis=0))
    assert nb % group == 0 and nb <= LANES and nb % tiles == 0
    kern = functools.partial(_moba_prompt_kernel, n_blocks=nb, group=group, tiles=tiles)
    qrows = tiles * blk
    whole = pl.BlockSpec((None, t, HEAD_DIM), lambda i, h, j: (i, 0, h))
    return pl.pallas_call(
        kern,
        grid=(b, N_HEADS, nb // tiles),
        in_specs=[pl.BlockSpec(memory_space=pltpu.SMEM),
                  pl.BlockSpec((2 * blk, blk), lambda i, h, j: (0, 0)),
                  pl.BlockSpec((None, qrows, HEAD_DIM), lambda i, h, j: (i, j, h)),
                  whole, whole, whole],
        out_specs=pl.BlockSpec((None, qrows, HEAD_DIM), lambda i, h, j: (i, j, h)),
        out_shape=jax.ShapeDtypeStruct((b, t, GROUP_W), F32),
        scratch_shapes=[pltpu.VMEM((t, HEAD_DIM), BF16), pltpu.VMEM((VT_ROWS, t), BF16),
                        pltpu.VMEM((nb, HEAD_DIM), F32), pltpu.VMEM((2 * blk, blk), F32),
                        pltpu.VMEM((nb, t), F32),
                        pltpu.VMEM((tiles, 2, group * blk, blk), F32),
                        pltpu.VMEM((tiles, 2, group * blk, blk), BF16)],
        compiler_params=pltpu.CompilerParams(dimension_semantics=("arbitrary", "arbitrary", "arbitrary"),
                                             vmem_limit_bytes=VMEM_LIMIT),
        name="moba_prompt",
    )(rel_bias, bkt, q, q, k, v)


def _moba_decode_kernel(pt_ref, relb_ref, bkt_ref, q_ref, kn_ref, vn_ref, *rest,
                        pages_per_step, n_blocks, n_new):
    del pt_ref
    npg = pages_per_step
    k_refs = rest[0:npg]
    v_refs = rest[npg:2 * npg]
    o_ref = rest[2 * npg]
    qbd_scr, m_scr, l_scr, acc_scr, kmean_scr, bias_scr = rest[2 * npg + 1:]
    jb = pl.program_id(1)
    blk = MOBA_BLOCK
    n_pairs = n_new * N_HEADS
    scale = HEAD_DIM ** -0.5
    bps = npg // 2

    lane1 = lax.broadcasted_iota(jnp.int32, (1, LANES), 1)
    far_row = jnp.zeros((1, LANES), F32)
    for h in range(N_HEADS):
        far_row = jnp.where((lane1 & 3) == h, relb_ref[h, NUM_BUCKETS - 1], far_row)

    @pl.when(jb == 0)
    def _():
        q = q_ref[...]
        rows = jnp.concatenate([jnp.broadcast_to(q[t:t + 1, :], (N_HEADS, GROUP_W)) for t in range(n_new)], axis=0)
        rp = lax.broadcasted_iota(jnp.int32, (n_pairs, GROUP_W), 0)
        cp = lax.broadcasted_iota(jnp.int32, (n_pairs, GROUP_W), 1)
        qbd = jnp.where((rp & 3) == (cp >> 7), rows, 0.0)
        qbd_scr[...] = jnp.zeros(qbd_scr.shape, F32)
        qbd_scr[0:n_pairs, :] = qbd
        bkt = bkt_ref[...]
        lane_h = lax.broadcasted_iota(jnp.int32, bkt.shape, 1) & 3
        b = jnp.zeros(bkt.shape, F32)
        for h in range(N_HEADS):
            b = jnp.where(lane_h == h, _bias_from_buckets(bkt, relb_ref, h), b)
        bias_scr[...] = b

    def load_page(ref):
        page = ref.shape[0] // N_HEADS
        return jnp.concatenate([ref[pl.ds(h, page, stride=N_HEADS), :] for h in range(N_HEADS)], axis=1)

    qbd = qbd_scr[...]
    j0 = pl.multiple_of(jb * bps, bps)
    k_all = jnp.concatenate([load_page(r) for r in k_refs], axis=0)
    kmean_scr[pl.ds(j0, bps), :] = jnp.mean(k_all.reshape(bps, blk, GROUP_W), axis=1)
    s = (_mm_nt(k_all, qbd) * scale).reshape(bps, blk, LANES)
    last_bias = jnp.where(jb == pl.num_programs(1) - 1, bias_scr[...], far_row)
    s = jnp.concatenate([s[0:bps - 1] + far_row, s[bps - 1:bps] + last_bias], axis=0)
    mj = jnp.max(s, axis=1, keepdims=True)
    p = jnp.exp(s - mj)
    m_scr[pl.ds(j0, bps), :] = mj.reshape(bps, LANES)
    l_scr[pl.ds(j0, bps), :] = jnp.sum(p, axis=1)
    p_t = p.reshape(bps * blk, LANES).T[0:n_pairs, :].astype(BF16)
    pvs = []
    for i in range(bps):
        vblk = jnp.concatenate([load_page(v_refs[2 * i]), load_page(v_refs[2 * i + 1])], axis=0)
        pvs.append(_mm(p_t[:, i * blk:(i + 1) * blk], vblk))
    acc_scr[pl.ds(pl.multiple_of(j0 * n_pairs, bps * n_pairs), bps * n_pairs), :] = jnp.concatenate(pvs, axis=0)

    @pl.when(jb == pl.num_programs(1) - 1)
    def _():
        gate = lax.dot_general(kmean_scr[...], qbd, (((1,), (1,)), ((), ())), precision=HI,
                               preferred_element_type=F32)
        row = lax.broadcasted_iota(jnp.int32, gate.shape, 0)
        g = gate
        sel = jnp.zeros(gate.shape, jnp.bool_)
        for _ in range(MOBA_TOPK):
            mx = jnp.max(g, axis=0, keepdims=True)
            idx = jnp.min(jnp.where(g == mx, row, n_blocks), axis=0, keepdims=True)
            hit = row == idx
            sel = sel | hit
            g = jnp.where(hit, NEG_INF, g)
        kn = jnp.concatenate([kn_ref[...], jnp.zeros((8 - n_new, GROUP_W), F32)], axis=0)
        s_own = _mm_nt(kn, qbd) * scale
        r8 = lax.broadcasted_iota(jnp.int32, (8, LANES), 0)
        l8 = lax.broadcasted_iota(jnp.int32, (8, LANES), 1)
        dist = (l8 >> 2) - r8
        own_bias = jnp.zeros((8, LANES), F32)
        for h in range(N_HEADS):
            for d in range(n_new):
                own_bias = jnp.where(((l8 & 3) == h) & (dist == d), relb_ref[h, d], own_bias)
        own_ok = (dist >= 0) & (r8 < n_new)
        s_own = jnp.where(own_ok, s_own + own_bias, NEG_INF)
        m_all = jnp.maximum(jnp.max(jnp.where(sel, m_scr[...], NEG_INF), axis=0, keepdims=True),
                            jnp.max(s_own, axis=0, keepdims=True))
        wgt = jnp.where(sel, jnp.exp(m_scr[...] - m_all), 0.0)
        p_own = jnp.exp(s_own - m_all)
        den = jnp.sum(wgt * l_scr[...], axis=0, keepdims=True) + jnp.sum(p_own, axis=0, keepdims=True)
        packed = jnp.concatenate([wgt, p_own, den, jnp.zeros((LANES - n_blocks - 9, LANES), F32)], axis=0)
        pk = packed.T
        out = jnp.zeros((n_pairs, GROUP_W), F32)
        for j in range(n_blocks):
            out = out + pk[0:n_pairs, j:j + 1] * acc_scr[j * n_pairs:(j + 1) * n_pairs, :]
        vn = vn_ref[...]
        for t in range(n_new):
            out = out + pk[0:n_pairs, n_blocks + t:n_blocks + t + 1] * vn[t:t + 1, :]
        out = out / pk[0:n_pairs, n_blocks + 8:n_blocks + 9]
        ph = lax.broadcasted_iota(jnp.int32, (n_pairs, HEAD_DIM), 0) & 3
        o16 = jnp.zeros((n_pairs, HEAD_DIM), F32)
        for h in range(N_HEADS):
            o16 = jnp.where(ph == h, out[:, h * HEAD_DIM:(h + 1) * HEAD_DIM], o16)
        for t in range(n_new):
            for h in range(N_HEADS):
                p = t * N_HEADS + h
                o_ref[t:t + 1, h * HEAD_DIM:(h + 1) * HEAD_DIM] = o16[p:p + 1, :]


def _moba_decode(q, k_new, v_new, cache_k, cache_v, page_table, rel_bias, pages_per_step=16):
    s, n_new, _ = q.shape
    n_pages = page_table.shape[1]
    page = cache_k.shape[1] // N_HEADS
    assert MOBA_BLOCK == 2 * page and n_pages % pages_per_step == 0 and pages_per_step % 16 == 0
    n_blocks = n_pages * page // MOBA_BLOCK
    assert n_blocks >= MOBA_TOPK and n_new <= 8
    npg = pages_per_step
    past = n_pages * page
    r = np.arange(MOBA_BLOCK)[:, None]
    lane_tok = (np.arange(LANES) >> 2)[None, :]
    bkt = jnp.asarray(_bucket_np(past + lane_tok - ((n_blocks - 1) * MOBA_BLOCK + r)))
    n_pairs = n_new * N_HEADS

    def page_spec(p):
        return pl.BlockSpec((None, page * N_HEADS, HEAD_DIM), lambda i, j, pt: (pt[i, j * npg + p], 0, 0))

    new_spec = pl.BlockSpec((None, n_new, GROUP_W), lambda i, j, pt: (i, 0, 0))
    kern = functools.partial(_moba_decode_kernel, pages_per_step=npg, n_blocks=n_blocks, n_new=n_new)
    grid_spec = pltpu.PrefetchScalarGridSpec(
        num_scalar_prefetch=1,
        grid=(s, n_pages // npg),
        in_specs=[pl.BlockSpec(memory_space=pltpu.SMEM),
                  pl.BlockSpec((MOBA_BLOCK, LANES), lambda i, j, pt: (0, 0)),
                  new_spec, new_spec, new_spec]
                 + [page_spec(p) for p in range(npg)] + [page_spec(p) for p in range(npg)],
        out_specs=new_spec,
        scratch_shapes=[pltpu.VMEM((LANES, GROUP_W), F32),
                        pltpu.VMEM((n_blocks, LANES), F32), pltpu.VMEM((n_blocks, LANES), F32),
                        pltpu.VMEM((n_blocks * n_pairs, GROUP_W), F32),
                        pltpu.VMEM((n_blocks, GROUP_W), F32),
                        pltpu.VMEM((MOBA_BLOCK, LANES), F32)],
    )
    return pl.pallas_call(
        kern,
        grid_spec=grid_spec,
        out_shape=jax.ShapeDtypeStruct((s, n_new, GROUP_W), F32),
        compiler_params=pltpu.CompilerParams(dimension_semantics=("arbitrary", "arbitrary"),
                                             vmem_limit_bytes=VMEM_LIMIT),
        name="moba_decode",
    )(page_table, rel_bias, bkt, q, k_new, v_new, *([cache_k] * npg), *([cache_v] * npg))


def _out_kernel(x_ref, od_ref, oa_ref, g1_ref, sh2_ref, sc2_ref, g2_ref, n2_ref, nf_ref,
                wo_ref, w1_ref, w2_ref, y_ref, *, ff_chunk):
    x = x_ref[...]
    mix = (jnp.dot(od_ref[...].astype(BF16), wo_ref[0:GROUP_W, :], preferred_element_type=F32)
           + jnp.dot(oa_ref[...].astype(BF16), wo_ref[GROUP_W:2 * GROUP_W, :], preferred_element_type=F32))
    x1 = x + g1_ref[...] * mix
    r = lax.rsqrt(jnp.mean(x1 * x1, axis=-1, keepdims=True) + EPS)
    h = ((x1 * r * n2_ref[...]) * (1.0 + sc2_ref[...]) + sh2_ref[...]).astype(BF16)
    d_ff = w1_ref.shape[1]
    ff = jnp.zeros(x.shape, F32)
    for c in range(d_ff // ff_chunk):
        a = jnp.dot(h, w1_ref[:, c * ff_chunk:(c + 1) * ff_chunk], preferred_element_type=F32)
        a = jnp.square(jnp.maximum(a, 0.0)).astype(BF16)
        ff = ff + jnp.dot(a, w2_ref[c * ff_chunk:(c + 1) * ff_chunk, :], preferred_element_type=F32)
    x2 = x1 + g2_ref[...] * ff
    r2 = lax.rsqrt(jnp.mean(x2 * x2, axis=-1, keepdims=True) + EPS)
    y_ref[...] = x2 * r2 * nf_ref[...]


def _out_mlp(x, o_d, o_a, g1, sh2, sc2, g2, n2, nf, w_out, w1, w2, tm, rows_per_mod, ff_chunk=1024):
    t, d = x.shape
    nt = t // tm
    if g1.shape[1] == 1:
        per = rows_per_mod // tm
        mod_spec = pl.BlockSpec((None, 1, d), lambda i: (i // per, 0, 0))
    else:
        mod_spec = pl.BlockSpec((None, tm, d), lambda i: (i, 0, 0))
    const = lambda i: (0, 0)
    row = lambda w: pl.BlockSpec((tm, w), lambda i: (i, 0))
    kern = functools.partial(_out_kernel, ff_chunk=ff_chunk)
    return pl.pallas_call(
        kern,
        grid=(nt,),
        in_specs=[row(d), row(GROUP_W), row(GROUP_W), mod_spec, mod_spec, mod_spec, mod_spec,
                  pl.BlockSpec((1, d), const), pl.BlockSpec((1, d), const),
                  pl.BlockSpec(w_out.shape, const), pl.BlockSpec(w1.shape, const),
                  pl.BlockSpec(w2.shape, const)],
        out_specs=row(d),
        out_shape=jax.ShapeDtypeStruct((t, d), F32),
        compiler_params=pltpu.CompilerParams(dimension_semantics=("arbitrary",),
                                             vmem_limit_bytes=VMEM_LIMIT),
        name="out_mlp",
    )(x, o_d, o_a, g1, sh2, sc2, g2, n2, nf, w_out, w1, w2)


def _split_w_in(w_in):
    ab0 = CONV_CH + GROUP_W
    w_main = jnp.concatenate([w_in[:, :ab0], w_in[:, ab0 + 2 * N_HEADS:]], axis=1).astype(BF16)
    w_ab = jnp.pad(w_in[:, ab0:ab0 + 2 * N_HEADS], ((0, 0), (0, LANES - 2 * N_HEADS)))
    w_ab_hi = w_ab.astype(BF16)
    w_ab_lo = (w_ab - w_ab_hi.astype(F32)).astype(BF16)
    return w_main, jnp.concatenate([w_ab_hi, w_ab_lo], axis=1)


def _lane_pad(v):
    return jnp.pad(v.astype(F32), (0, LANES - v.shape[0])).reshape(1, LANES)


def kernel(x_prompt, x_sample, cache_k, cache_v, state_delta, state_conv, page_table, c_prompt, c_sample,
           norm1_w, norm2_w, normf_w, w_ada, b_ada, w_in, w_out, conv_w, a_log, dt_bias, onorm_w,
           rel_bias, w_mlp1, w_mlp2):
    bp, tp, d = x_prompt.shape
    bs, ts, _ = x_sample.shape
    depth = w_in.shape[0]
    assert depth == 1 and d == 2 * GROUP_W
    l = 0
    n_pool, page = cache_k.shape[1], cache_k.shape[2]

    mod = _modulation(jnp.concatenate([c_prompt, c_sample], axis=0), w_ada[l], b_ada[l])
    mod = mod.reshape(bp + bs, 6, d)
    mod_p = [mod[:bp, i].reshape(bp, 1, d) for i in range(6)]
    tm_s = bs * ts
    mod_s = [jnp.broadcast_to(mod[bp:, i][:, None, :], (bs, ts, d)).reshape(1, tm_s, d) for i in range(6)]

    w_main, w_ab = _split_w_in(w_in[l])
    alog, dtb = _lane_pad(a_log[l]), _lane_pad(dt_bias[l])
    n1 = norm1_w[l].reshape(1, d)
    n2 = norm2_w[l].reshape(1, d)
    nf = normf_w.reshape(1, d)
    wo = w_out[l].astype(BF16)
    w1 = w_mlp1[l].astype(BF16)
    w2 = w_mlp2[l].astype(BF16)
    cw = conv_w[l]
    onw = onorm_w[l].reshape(1, HEAD_DIM)

    tm = 512
    xp = x_prompt.reshape(bp * tp, d)
    qkv_p, z_p, qa_p, ka_p, va_p, gb_p, gbt_p = _inproj(
        xp, mod_p[0], mod_p[1], n1, w_main, w_ab, alog, dtb, tm, tp)
    n_chunks = 8
    od_p, s_p = _gdn(qkv_p.reshape(bp, tp, CONV_CH), gb_p.reshape(bp, tp, LANES),
                     gbt_p.reshape(8, bp, tp).transpose(1, 0, 2), z_p.reshape(bp, tp, GROUP_W),
                     jnp.zeros((bp, CONV_W - 1, CONV_CH), F32),
                     jnp.zeros((bp, N_HEADS, HEAD_DIM, HEAD_DIM), F32), cw, onw, n_chunks, 1)
    oa_p = _moba_prompt(qa_p.reshape(bp, tp, GROUP_W), ka_p.reshape(bp, tp, GROUP_W),
                        va_p.reshape(bp, tp, GROUP_W), rel_bias)
    y_p = _out_mlp(xp, od_p.reshape(bp * tp, GROUP_W), oa_p.reshape(bp * tp, GROUP_W),
                   mod_p[2], mod_p[3], mod_p[4], mod_p[5], n2, nf, wo, w1, w2, tm, tp)

    xs = x_sample.reshape(tm_s, d)
    qkv_s, z_s, qa_s, ka_s, va_s, gb_s, gbt_s = _inproj(
        xs, mod_s[0], mod_s[1], n1, w_main, w_ab, alog, dtb, tm_s, tm_s)
    gb_s3 = jnp.pad(gb_s.reshape(bs, ts, LANES), ((0, 0), (0, CHUNK - ts), (0, 0)))
    gbt_s3 = jnp.pad(gbt_s.reshape(8, bs, ts).transpose(1, 0, 2), ((0, 0), (0, 0), (0, CHUNK - ts)))
    od_s, s_s = _gdn(qkv_s.reshape(bs, ts, CONV_CH), gb_s3, gbt_s3, z_s.reshape(bs, ts, GROUP_W),
                     state_conv[l], state_delta[l], cw, onw, 1, next(n for n in (4, 2, 1) if bs % n == 0))
    oa_s = _moba_decode(qa_s.reshape(bs, ts, GROUP_W), ka_s.reshape(bs, ts, GROUP_W),
                        va_s.reshape(bs, ts, GROUP_W),
                        cache_k.reshape(depth * n_pool, page * N_HEADS, HEAD_DIM),
                        cache_v.reshape(depth * n_pool, page * N_HEADS, HEAD_DIM),
                        page_table, rel_bias)
    y_s = _out_mlp(xs, od_s.reshape(tm_s, GROUP_W), oa_s.reshape(tm_s, GROUP_W),
                   mod_s[2], mod_s[3], mod_s[4], mod_s[5], n2, nf, wo, w1, w2, tm_s, tm_s)

    shp_p = (1, bp, tp, N_HEADS, HEAD_DIM)
    shp_s = (1, bs, ts, N_HEADS, HEAD_DIM)
    qkv_p3 = qkv_p.reshape(bp, tp, CONV_CH)
    qkv_s3 = qkv_s.reshape(bs, ts, CONV_CH)
    conv_s = jnp.concatenate([state_conv[l], qkv_s3], axis=1)[:, ts:]
    return (y_p.reshape(bp, tp, d), y_s.reshape(bs, ts, d),
            ka_p.reshape(shp_p), va_p.reshape(shp_p), s_p[None], qkv_p3[:, tp - (CONV_W - 1):][None],
            ka_s.reshape(shp_s), va_s.reshape(shp_s), s_s[None], conv_s[None])
```

```python
import functools
import math

import numpy as np
import jax
import jax.numpy as jnp
from jax import lax
from jax.experimental import pallas as pl
from jax.experimental.pallas import tpu as pltpu

F32 = jnp.float32
BF16 = jnp.bfloat16
HI = lax.Precision.HIGHEST
NEG_INF = float("-inf")
LOG2E = math.log2(math.e)

EPS = 1e-6
N_HEADS = 4
HEAD_DIM = 128
GROUP_W = N_HEADS * HEAD_DIM
CONV_W = 4
CONV_CH = 3 * GROUP_W
CHUNK = 64
MIN_CHUNK = 16
MOBA_BLOCK = 256
MOBA_TOPK = 3
NUM_BUCKETS = 32
MAX_DISTANCE = 128
LANES = 128
VT_ROWS = HEAD_DIM + 16
VMEM_LIMIT = 56 * 1024 * 1024


def _bucket_np(dist):
    n = np.maximum(dist, 0)
    max_exact = NUM_BUCKETS // 2
    nf = np.maximum(n, 1).astype(np.float32)
    large = max_exact + (np.log(nf / max_exact) / math.log(MAX_DISTANCE / max_exact)
                         * (NUM_BUCKETS - max_exact)).astype(np.int32)
    large = np.minimum(large, NUM_BUCKETS - 1)
    return np.where(n < max_exact, n, large).astype(np.int32)


def _mm(a, b):
    return jnp.dot(a.astype(BF16), b.astype(BF16), preferred_element_type=F32)


def _mm_nt(a, b):
    return lax.dot_general(a.astype(BF16), b.astype(BF16), (((1,), (1,)), ((), ())),
                           preferred_element_type=F32)


def _mm_tn(a, b):
    return lax.dot_general(a.astype(BF16), b.astype(BF16), (((0,), (0,)), ((), ())),
                           preferred_element_type=F32)


def _silu(x):
    return x * jax.nn.sigmoid(x)


def _softplus(x):
    return jnp.maximum(x, 0.0) + jnp.log1p(jnp.exp(-jnp.abs(x)))


def _mod_kernel(c_ref, w_ref, b_ref, o_ref):
    o_ref[...] = jnp.dot(_silu(c_ref[...]), w_ref[...], precision=HI,
                         preferred_element_type=F32) + b_ref[...]


def _modulation(c, w, b, tn=1536):
    m, d = c.shape
    n = w.shape[1]
    return pl.pallas_call(
        _mod_kernel,
        grid=(n // tn,),
        in_specs=[pl.BlockSpec((m, d), lambda j: (0, 0)),
                  pl.BlockSpec((d, tn), lambda j: (0, j)),
                  pl.BlockSpec((1, tn), lambda j: (0, j))],
        out_specs=pl.BlockSpec((m, tn), lambda j: (0, j)),
        out_shape=jax.ShapeDtypeStruct((m, n), F32),
        compiler_params=pltpu.CompilerParams(dimension_semantics=("arbitrary",),
                                             vmem_limit_bytes=VMEM_LIMIT),
        name="modulation",
    )(c, w, b.reshape(1, n))


def _inproj_kernel(x_ref, sh_ref, sc_ref, n1_ref, wm_ref, wab_ref, alog_ref, dtb_ref,
                   qkv_ref, z_ref, qa_ref, ka_ref, va_ref, gb_ref, gbt_ref, ka4_ref, va4_ref):
    x = x_ref[...]
    tm = x.shape[0]
    r = lax.rsqrt(jnp.mean(x * x, axis=-1, keepdims=True) + EPS)
    h = (x * r * n1_ref[...]) * (1.0 + sc_ref[...]) + sh_ref[...]
    hb = h.astype(BF16)
    off = 0
    for ref in (qkv_ref, z_ref, qa_ref, ka_ref, va_ref):
        w = ref.shape[-1]
        res = jnp.dot(hb, wm_ref[:, off:off + w], preferred_element_type=F32)
        ref[...] = res
        off += w
        for out4 in [r4 for r4, src in ((ka4_ref, ka_ref), (va4_ref, va_ref)) if src is ref]:
            for hd in range(N_HEADS):
                out4[pl.ds(hd, tm, stride=N_HEADS), :] = res[:, hd * HEAD_DIM:(hd + 1) * HEAD_DIM]
    h_lo = (h - hb.astype(F32)).astype(BF16)
    ab2 = jnp.dot(hb, wab_ref[...], preferred_element_type=F32)
    ab = (ab2[:, 0:LANES] + ab2[:, LANES:2 * LANES]
          + jnp.dot(h_lo, wab_ref[:, 0:LANES], preferred_element_type=F32))
    lane = lax.broadcasted_iota(jnp.int32, ab.shape, 1)
    g = -jnp.exp(alog_ref[...]) * _softplus(ab + dtb_ref[...])
    gb = jnp.where(lane < N_HEADS, g, jax.nn.sigmoid(ab))
    gb_ref[...] = gb
    gbt_ref[...] = gb.T[0:8, :]


def _inproj(x, shift, scale, n1, w_main, w_ab, alog, dtb, tm, rows_per_mod):
    t, d = x.shape
    nt = t // tm
    if shift.shape[1] == 1:
        per = rows_per_mod // tm
        mod_spec = pl.BlockSpec((None, 1, d), lambda i: (i // per, 0, 0))
    else:
        mod_spec = pl.BlockSpec((None, tm, d), lambda i: (i, 0, 0))
    const = lambda i: (0, 0)
    widths = (CONV_CH, GROUP_W, GROUP_W, GROUP_W, GROUP_W)
    outs = [jax.ShapeDtypeStruct((t, w), F32) for w in widths]
    outs += [jax.ShapeDtypeStruct((t, LANES), F32), jax.ShapeDtypeStruct((8, t), F32)]
    outs += [jax.ShapeDtypeStruct((t * N_HEADS, HEAD_DIM), F32)] * 2
    out_specs = [pl.BlockSpec((tm, w), lambda i: (i, 0)) for w in widths]
    out_specs += [pl.BlockSpec((tm, LANES), lambda i: (i, 0)), pl.BlockSpec((8, tm), lambda i: (0, i))]
    out_specs += [pl.BlockSpec((tm * N_HEADS, HEAD_DIM), lambda i: (i, 0))] * 2
    in_specs = [pl.BlockSpec((tm, d), lambda i: (i, 0)), mod_spec, mod_spec,
                pl.BlockSpec((1, d), const),
                pl.BlockSpec(w_main.shape, const),
                pl.BlockSpec(w_ab.shape, const),
                pl.BlockSpec((1, LANES), const), pl.BlockSpec((1, LANES), const)]
    return pl.pallas_call(
        _inproj_kernel,
        grid=(nt,),
        in_specs=in_specs,
        out_specs=out_specs,
        out_shape=outs,
        compiler_params=pltpu.CompilerParams(dimension_semantics=("arbitrary",),
                                             vmem_limit_bytes=VMEM_LIMIT),
        name="inproj",
    )(x, shift, scale, n1, w_main, w_ab, alog, dtb)


def _gdn_kernel(x_ref, gb_ref, gbt_ref, z_ref, cs_ref, s0_ref, cw_ref, onw_ref,
                o_ref, sn_ref, xbuf, s_scr, *, t_real, n_chunks, n_seq, chunk):
    CHUNK, STACK = chunk, N_HEADS * chunk
    shift = chunk.bit_length() - 1
    ti = pl.program_id(1)
    t_tile = n_chunks * CHUNK

    @pl.when(ti == 0)
    def _():
        for n in range(n_seq):
            xbuf[n, 0:8, :] = jnp.zeros((8, CONV_CH), F32)
            xbuf[n, 5:8, :] = cs_ref[n]
            if t_real < t_tile:
                xbuf[n, 8:8 + t_tile, :] = jnp.zeros((t_tile, CONV_CH), F32)
        s_scr[...] = s0_ref[...]

    for n in range(n_seq):
        xbuf[n, 8:8 + t_real, :] = x_ref[n]

    ri = lax.broadcasted_iota(jnp.int32, (STACK, STACK), 0)
    ci = lax.broadcasted_iota(jnp.int32, (STACK, STACK), 1)
    same = (ri >> shift) == (ci >> shift)
    causal = same & ((ri & (chunk - 1)) >= (ci & (chunk - 1)))
    strict = same & ((ri & (chunk - 1)) > (ci & (chunk - 1)))
    eye = (ri == ci).astype(F32)
    l_st = (lax.broadcasted_iota(jnp.int32, (STACK, CHUNK), 1)
            <= (lax.broadcasted_iota(jnp.int32, (STACK, CHUNK), 0) & (chunk - 1))).astype(F32)
    u_st = (lax.broadcasted_iota(jnp.int32, (CHUNK, STACK), 0)
            <= (lax.broadcasted_iota(jnp.int32, (CHUNK, STACK), 1) & (chunk - 1))).astype(F32)
    row_head = lax.broadcasted_iota(jnp.int32, (STACK, LANES), 0) >> shift
    lane_id = lax.broadcasted_iota(jnp.int32, (STACK, LANES), 1)
    pick_g = lane_id == row_head
    pick_b = lane_id == row_head + N_HEADS
    col_head8 = lax.broadcasted_iota(jnp.int32, (8, STACK), 1) >> shift
    sub8 = lax.broadcasted_iota(jnp.int32, (8, STACK), 0)
    cw = cw_ref[...]
    onw = onw_ref[...]

    def stack(a):
        return jnp.concatenate([a[:, h * HEAD_DIM:(h + 1) * HEAD_DIM] for h in range(N_HEADS)], axis=0)

    def l2n(a):
        return a * lax.rsqrt(jnp.sum(a * a, axis=-1, keepdims=True) + EPS)

    items = [(n, c) for n in range(n_seq) for c in range(n_chunks)]

    def conv_qkv(n, c):
        base = 8 + c * CHUNK
        y = xbuf[n, base - 3:base - 3 + CHUNK, :] * cw[0:1, :]
        for i in range(1, CONV_W):
            y = y + xbuf[n, base - 3 + i:base - 3 + i + CHUNK, :] * cw[i:i + 1, :]
        y = _silu(y)
        q = l2n(stack(y[:, 0:GROUP_W])) * (HEAD_DIM ** -0.5)
        k = l2n(stack(y[:, GROUP_W:2 * GROUP_W]))
        v = stack(y[:, 2 * GROUP_W:3 * GROUP_W])
        return q, k, v

    def gates(n, c):
        gb = gb_ref[n, c * CHUNK:(c + 1) * CHUNK, :]
        gbt = gbt_ref[n, :, c * CHUNK:(c + 1) * CHUNK]
        gcum = jnp.dot(l_st, gb, precision=HI, preferred_element_type=F32)
        g_col = jnp.sum(jnp.where(pick_g, gcum, 0.0), axis=1, keepdims=True)
        g_last = gcum[CHUNK - 1:CHUNK, :]
        g_tot = jnp.sum(jnp.where(pick_g, jnp.broadcast_to(g_last, (STACK, LANES)), 0.0),
                        axis=1, keepdims=True)
        gb_st = jnp.concatenate([gb] * N_HEADS, axis=0)
        beta = jnp.sum(jnp.where(pick_b, gb_st, 0.0), axis=1, keepdims=True)
        grow_all = jnp.dot(gbt, u_st, precision=HI, preferred_element_type=F32)
        g_row = jnp.sum(jnp.where(sub8 == col_head8, grow_all, 0.0), axis=0, keepdims=True)
        decay = jnp.exp(jnp.where(causal, g_col - g_row, NEG_INF))
        return g_col, g_tot, g_last, beta, decay

    qkv = [conv_qkv(n, c) for n, c in items]
    gts = [gates(n, c) for n, c in items]
    kbs = [k.astype(BF16) for _, k, _ in qkv]
    a_mats = [jnp.where(strict, g[3] * _mm_nt(kb, kb) * g[4], 0.0) for kb, g in zip(kbs, gts)]
    invs = [eye - a for a in a_mats]
    pws = a_mats
    for _ in range(shift - 1):
        pws = [_mm(p, p) for p in pws]
        invs = [x + _mm(x, p) for x, p in zip(invs, pws)]
    resids = [eye - jnp.dot(eye + a, x, precision=HI, preferred_element_type=F32) for a, x in zip(a_mats, invs)]
    invs = [x + _mm(x, r) for x, r in zip(invs, resids)]
    exp_gs = [jnp.exp(g[0]) for g in gts]
    sols = [_mm(x, jnp.concatenate([v * g[3], k * (g[3] * eg)], axis=1))
            for x, (_, k, v), g, eg in zip(invs, qkv, gts, exp_gs)]
    qks = [_mm_nt(q, kb) * g[4] for (q, _, _), kb, g in zip(qkv, kbs, gts)]
    q_decs = [q * eg for (q, _, _), eg in zip(qkv, exp_gs)]
    k_tails = [k * jnp.exp(g[1] - g[0]) for (_, k, _), g in zip(qkv, gts)]

    for idx, (n, c) in enumerate(items):
        u0 = sols[idx][:, 0:HEAD_DIM]
        w = sols[idx][:, HEAD_DIM:2 * HEAD_DIM]
        us, qs = [], []
        for h in range(N_HEADS):
            sl = slice(h * CHUNK, (h + 1) * CHUNK)
            wq = jnp.concatenate([w[sl], q_decs[idx][sl]], axis=0)
            r = _mm(wq, s_scr[n, h])
            us.append(u0[sl] - r[0:CHUNK])
            qs.append(r[CHUNK:2 * CHUNK])
        u = jnp.concatenate(us, axis=0)
        o = jnp.concatenate(qs, axis=0) + _mm(qks[idx], u)
        for h in range(N_HEADS):
            sl = slice(h * CHUNK, (h + 1) * CHUNK)
            gl = jnp.exp(gts[idx][2][:, h:h + 1])
            s_scr[n, h] = s_scr[n, h] * gl + _mm_tn(k_tails[idx][sl], u[sl])

        o = o * lax.rsqrt(jnp.mean(o * o, axis=-1, keepdims=True) + EPS) * onw
        rows = min(CHUNK, t_real - c * CHUNK)
        for h in range(N_HEADS):
            zg = _silu(z_ref[n, c * CHUNK:c * CHUNK + rows, h * HEAD_DIM:(h + 1) * HEAD_DIM])
            o_ref[n, c * CHUNK:c * CHUNK + rows, h * HEAD_DIM:(h + 1) * HEAD_DIM] = \
                o[h * CHUNK:h * CHUNK + rows] * zg

    if t_real == t_tile:
        for n in range(n_seq):
            xbuf[n, 5:8, :] = xbuf[n, 8 + t_tile - 3:8 + t_tile, :]

    @pl.when(ti == pl.num_programs(1) - 1)
    def _():
        sn_ref[...] = s_scr[...]


def _gdn(qkv, gb, gbt, z, conv_state, s0, conv_w, onorm_w, n_chunks, n_seq, chunk):
    b, t, _ = qkv.shape
    t_tile = n_chunks * chunk
    t_real = min(t, t_tile)
    nt = max(1, t // t_tile)
    assert b % n_seq == 0 and chunk >= MIN_CHUNK and chunk & (chunk - 1) == 0
    kern = functools.partial(_gdn_kernel, t_real=t_real, n_chunks=n_chunks, n_seq=n_seq, chunk=chunk)
    return pl.pallas_call(
        kern,
        grid=(b // n_seq, nt),
        in_specs=[pl.BlockSpec((n_seq, t_real, CONV_CH), lambda i, j: (i, j, 0)),
                  pl.BlockSpec((n_seq, t_tile, LANES), lambda i, j: (i, j, 0)),
                  pl.BlockSpec((n_seq, 8, t_tile), lambda i, j: (i, 0, j)),
                  pl.BlockSpec((n_seq, t_real, GROUP_W), lambda i, j: (i, j, 0)),
                  pl.BlockSpec((n_seq, CONV_W - 1, CONV_CH), lambda i, j: (i, 0, 0)),
                  pl.BlockSpec((n_seq, N_HEADS, HEAD_DIM, HEAD_DIM), lambda i, j: (i, 0, 0, 0)),
                  pl.BlockSpec((CONV_W, CONV_CH), lambda i, j: (0, 0)),
                  pl.BlockSpec((1, HEAD_DIM), lambda i, j: (0, 0))],
        out_specs=[pl.BlockSpec((n_seq, t_real, GROUP_W), lambda i, j: (i, j, 0)),
                   pl.BlockSpec((n_seq, N_HEADS, HEAD_DIM, HEAD_DIM), lambda i, j: (i, 0, 0, 0))],
        out_shape=[jax.ShapeDtypeStruct((b, t, GROUP_W), F32),
                   jax.ShapeDtypeStruct((b, N_HEADS, HEAD_DIM, HEAD_DIM), F32)],
        scratch_shapes=[pltpu.VMEM((n_seq, 8 + t_tile, CONV_CH), F32),
                        pltpu.VMEM((n_seq, N_HEADS, HEAD_DIM, HEAD_DIM), F32)],
        compiler_params=pltpu.CompilerParams(dimension_semantics=("arbitrary", "arbitrary"),
                                             vmem_limit_bytes=VMEM_LIMIT),
        name="gated_delta",
    )(qkv, gb, gbt, z, conv_state, s0, conv_w, onorm_w)


def _bias_from_buckets(bkt, relb_ref, h):
    def body(t, b):
        return jnp.where(bkt == t, relb_ref[h, t], b)
    return lax.fori_loop(0, NUM_BUCKETS, body, jnp.zeros(bkt.shape, F32))


def _moba_prompt_kernel(relb_ref, bkt_ref, q_ref, qall_ref, k_ref, v_ref, o_ref,
                        kb_scr, vt_scr, kmean_scr, bias_scr, sel_scr, s_buf, p_buf, *, n_blocks, group, tiles):
    h = pl.program_id(1)
    step = pl.program_id(2)
    blk = MOBA_BLOCK
    scale = HEAD_DIM ** -0.5

    @pl.when(step == 0)
    def _():
        kb_scr[...] = k_ref[...].astype(BF16)
        for j in range(n_blocks):
            vt_scr[0:HEAD_DIM, j * blk:(j + 1) * blk] = v_ref[j * blk:(j + 1) * blk, :].T.astype(BF16)
            kmean_scr[j:j + 1, :] = jnp.mean(k_ref[j * blk:(j + 1) * blk, :], axis=0, keepdims=True)
        tail = (VT_ROWS - HEAD_DIM, k_ref.shape[0])
        vt_scr[HEAD_DIM:VT_ROWS, :] = jnp.where(lax.broadcasted_iota(jnp.int32, tail, 0) == 0, 1.0, 0.0).astype(BF16)
        bias_scr[...] = _bias_from_buckets(bkt_ref[...], relb_ref, h) * LOG2E
        t_all = qall_ref.shape[0]
        gate = lax.dot_general(kmean_scr[...], qall_ref[...], (((1,), (1,)), ((), ())), precision=HI,
                               preferred_element_type=F32)
        brow = lax.broadcasted_iota(jnp.int32, gate.shape, 0)
        own = lax.broadcasted_iota(jnp.int32, gate.shape, 1) >> (blk.bit_length() - 1)
        g = jnp.where(brow < own, gate, NEG_INF)
        sel_t = jnp.zeros(gate.shape, F32)
        for s in range(MOBA_TOPK):
            mx = jnp.max(g, axis=0, keepdims=True)
            idx = jnp.min(jnp.where(g == mx, brow, n_blocks), axis=0, keepdims=True)
            hit = brow == idx
            sel_t = jnp.maximum(sel_t, jnp.where(hit & (own > s), 1.0, 0.0))
            g = jnp.where(hit, NEG_INF, g)
        sel_scr[...] = sel_t

    tq = range(tiles)
    qis = [step * tiles + t for t in tq]
    qb_all = q_ref[...].astype(BF16)
    qbs = [q_ref[t * blk:(t + 1) * blk, :].astype(BF16) for t in tq]
    far_bias = relb_ref[h, NUM_BUCKETS - 1]
    own0 = [pl.multiple_of(qis[t] * blk, blk) for t in tq]
    prev0 = [pl.multiple_of(jnp.maximum(qis[t] - 1, 0) * blk, blk) for t in tq]
    sels = [sel_scr[:, pl.ds(own0[t], blk)] for t in tq]
    brow = lax.broadcasted_iota(jnp.int32, (n_blocks, blk), 0)

    def block_row(sel, j):
        return jnp.sum(jnp.where(brow == j, sel, 0.0), axis=0, keepdims=True)

    prev_chosen = [block_row(sels[t], qis[t] - 1) > 0.0 for t in tq]
    sel_far = [jnp.where(brow == qis[t] - 1, 0.0, sels[t]) for t in tq]

    ki = lax.broadcasted_iota(jnp.int32, (blk, blk), 0)
    qcol = lax.broadcasted_iota(jnp.int32, (blk, blk), 1)
    kds = [jnp.concatenate([kb_scr[pl.ds(prev0[t], blk), :], kb_scr[pl.ds(own0[t], blk), :]], axis=0) for t in tq]
    vds = [jnp.concatenate([vt_scr[:, pl.ds(prev0[t], blk)], vt_scr[:, pl.ds(own0[t], blk)]], axis=1) for t in tq]
    t0s = [_mm_nt(kds[t], qbs[t]) * (scale * LOG2E) + bias_scr[...] for t in tq]
    keeps = [jnp.concatenate([jnp.broadcast_to(prev_chosen[t], (blk, blk)), ki <= qcol], axis=0) for t in tq]
    t0s = [jnp.where(keeps[t], t0s[t], NEG_INF) for t in tq]
    m0s = [jnp.max(t0s[t], axis=0, keepdims=True) for t in tq]
    acc0s = [jnp.dot(vds[t], jnp.exp2(t0s[t] - m0s[t]).astype(BF16), preferred_element_type=F32)
             for t in tq]

    gw = group * blk
    n_groups = n_blocks // group
    far_t = far_bias * LOG2E

    def scores(g, slot):
        start = pl.multiple_of(g * gw, gw)
        s = _mm_nt(kb_scr[pl.ds(start, gw), :], qb_all)
        for t in tq:
            s_buf[t, slot] = s[:, t * blk:(t + 1) * blk]

    def flush(accs, alphas, g, slot):
        start = pl.multiple_of(g * gw, gw)
        vg = vt_scr[:, pl.ds(start, gw)]
        return [alphas[t] * accs[t] + jnp.dot(vg, p_buf[t, slot], preferred_element_type=F32) for t in tq]

    def softmax_step(ms, g, slot):
        new_ms, alphas = [], []
        for t in tq:
            s = s_buf[t, slot]
            tt = jnp.concatenate(
                [s[u * blk:(u + 1) * blk] * (scale * LOG2E)
                 + jnp.where(block_row(sel_far[t], g * group + u) > 0.0, far_t, NEG_INF) for u in range(group)], axis=0)
            m_new = jnp.maximum(ms[t], jnp.max(tt, axis=0, keepdims=True))
            p_buf[t, slot] = jnp.exp2(tt - m_new).astype(BF16)
            new_ms.append(m_new)
            alphas.append(jnp.exp2(ms[t] - m_new))
        return new_ms, alphas

    scores(0, 0)
    for t in tq:
        p_buf[t, 1] = jnp.zeros((gw, blk), BF16)
    last = n_groups - 1

    def body(i, carry):
        ms, accs, alpha_prev, g_prev = carry
        ms, accs, alpha_prev = list(ms), list(accs), list(alpha_prev)
        g_a = jnp.minimum(2 * i, last)
        g_b = jnp.minimum(2 * i + 1, last)
        scores(g_b, 1)
        accs = flush(accs, alpha_prev, g_prev, 1)
        ms, alpha_a = softmax_step(ms, 2 * i, 0)
        scores(jnp.minimum(2 * i + 2, last), 0)
        accs = flush(accs, alpha_a, g_a, 0)
        ms, alpha_b = softmax_step(ms, 2 * i + 1, 1)
        return tuple(ms), tuple(accs), tuple(alpha_b), g_b

    n_needed = (qis[-1] + group - 2) // group
    init = (tuple(m0s), tuple(acc0s), tuple(jnp.ones((1, blk), F32) for _ in tq), 0)
    _, accs, alpha_prev, g_prev = lax.fori_loop(0, (n_needed + 1) // 2, body, init)
    accs = flush(list(accs), list(alpha_prev), g_prev, 1)
    for t in tq:
        o_t = accs[t][0:HEAD_DIM] / accs[t][HEAD_DIM:HEAD_DIM + 1]
        o_ref[t * blk:(t + 1) * blk, :] = o_t.T


def _moba_prompt(q, k, v, rel_bias, group=4, tiles=2):
    b, t, _ = q.shape
    blk = MOBA_BLOCK
    nb = t // blk
    kk = np.arange(blk)[:, None]
    qq = np.arange(blk)[None, :]
    bkt = jnp.asarray(np.concatenate([_bucket_np(blk + qq - kk), _bucket_np(qq - kk)], axis=0))
    assert nb % group == 0 and nb <= LANES and nb % tiles == 0
    kern = functools.partial(_moba_prompt_kernel, n_blocks=nb, group=group, tiles=tiles)
    qrows = tiles * blk
    whole = pl.BlockSpec((None, t, HEAD_DIM), lambda i, h, j: (i, 0, h))
    return pl.pallas_call(
        kern,
        grid=(b, N_HEADS, nb // tiles),
        in_specs=[pl.BlockSpec(memory_space=pltpu.SMEM),
                  pl.BlockSpec((2 * blk, blk), lambda i, h, j: (0, 0)),
                  pl.BlockSpec((None, qrows, HEAD_DIM), lambda i, h, j: (i, j, h)),
                  whole, whole, whole],
        out_specs=pl.BlockSpec((None, qrows, HEAD_DIM), lambda i, h, j: (i, j, h)),
        out_shape=jax.ShapeDtypeStruct((b, t, GROUP_W), F32),
        scratch_shapes=[pltpu.VMEM((t, HEAD_DIM), BF16), pltpu.VMEM((VT_ROWS, t), BF16),
                        pltpu.VMEM((nb, HEAD_DIM), F32), pltpu.VMEM((2 * blk, blk), F32),
                        pltpu.VMEM((nb, t), F32),
                        pltpu.VMEM((tiles, 2, group * blk, blk), F32),
                        pltpu.VMEM((tiles, 2, group * blk, blk), BF16)],
        compiler_params=pltpu.CompilerParams(dimension_semantics=("arbitrary", "arbitrary", "arbitrary"),
                                             vmem_limit_bytes=VMEM_LIMIT),
        name="moba_prompt",
    )(rel_bias, bkt, q, q, k, v)


def _moba_decode_kernel(pt_ref, relb_ref, bkt_ref, q_ref, kn_ref, vn_ref, *rest,
                        pages_per_step, n_blocks, n_new):
    del pt_ref
    npg = pages_per_step
    k_refs = rest[0:npg]
    v_refs = rest[npg:2 * npg]
    o_ref = rest[2 * npg]
    qbd_scr, m_scr, l_scr, acc_scr, kmean_scr, bias_scr = rest[2 * npg + 1:]
    jb = pl.program_id(1)
    blk = MOBA_BLOCK
    n_pairs = n_new * N_HEADS
    scale = HEAD_DIM ** -0.5
    bps = npg // 2

    lane1 = lax.broadcasted_iota(jnp.int32, (1, LANES), 1)
    far_row = jnp.zeros((1, LANES), F32)
    for h in range(N_HEADS):
        far_row = jnp.where((lane1 & 3) == h, relb_ref[h, NUM_BUCKETS - 1], far_row)

    @pl.when(jb == 0)
    def _():
        q = q_ref[...]
        rows = jnp.concatenate([jnp.broadcast_to(q[t:t + 1, :], (N_HEADS, GROUP_W)) for t in range(n_new)], axis=0)
        rp = lax.broadcasted_iota(jnp.int32, (n_pairs, GROUP_W), 0)
        cp = lax.broadcasted_iota(jnp.int32, (n_pairs, GROUP_W), 1)
        qbd = jnp.where((rp & 3) == (cp >> 7), rows, 0.0)
        qbd_scr[...] = jnp.zeros(qbd_scr.shape, F32)
        qbd_scr[0:n_pairs, :] = qbd
        bkt = bkt_ref[...]
        lane_h = lax.broadcasted_iota(jnp.int32, bkt.shape, 1) & 3
        b = jnp.zeros(bkt.shape, F32)
        for h in range(N_HEADS):
            b = jnp.where(lane_h == h, _bias_from_buckets(bkt, relb_ref, h), b)
        bias_scr[...] = b

    def load_page(ref):
        page = ref.shape[0] // N_HEADS
        return jnp.concatenate([ref[pl.ds(h, page, stride=N_HEADS), :] for h in range(N_HEADS)], axis=1)

    qbd = qbd_scr[...]
    j0 = pl.multiple_of(jb * bps, bps)
    k_all = jnp.concatenate([load_page(r) for r in k_refs], axis=0)
    kmean_scr[pl.ds(j0, bps), :] = jnp.mean(k_all.reshape(bps, blk, GROUP_W), axis=1)
    s = (_mm_nt(k_all, qbd) * scale).reshape(bps, blk, LANES)
    last_bias = jnp.where(jb == pl.num_programs(1) - 1, bias_scr[...], far_row)
    s = jnp.concatenate([s[0:bps - 1] + far_row, s[bps - 1:bps] + last_bias], axis=0)
    mj = jnp.max(s, axis=1, keepdims=True)
    p = jnp.exp(s - mj)
    m_scr[pl.ds(j0, bps), :] = mj.reshape(bps, LANES)
    l_scr[pl.ds(j0, bps), :] = jnp.sum(p, axis=1)
    p_t = p.reshape(bps * blk, LANES).T[0:n_pairs, :].astype(BF16)
    pvs = []
    for i in range(bps):
        vblk = jnp.concatenate([load_page(v_refs[2 * i]), load_page(v_refs[2 * i + 1])], axis=0)
        pvs.append(_mm(p_t[:, i * blk:(i + 1) * blk], vblk))
    acc_scr[pl.ds(pl.multiple_of(j0 * n_pairs, bps * n_pairs), bps * n_pairs), :] = jnp.concatenate(pvs, axis=0)

    @pl.when(jb == pl.num_programs(1) - 1)
    def _():
        gate = lax.dot_general(kmean_scr[...], qbd, (((1,), (1,)), ((), ())), precision=HI,
                               preferred_element_type=F32)
        row = lax.broadcasted_iota(jnp.int32, gate.shape, 0)
        g = gate
        sel = jnp.zeros(gate.shape, jnp.bool_)
        for _ in range(MOBA_TOPK):
            mx = jnp.max(g, axis=0, keepdims=True)
            idx = jnp.min(jnp.where(g == mx, row, n_blocks), axis=0, keepdims=True)
            hit = row == idx
            sel = sel | hit
            g = jnp.where(hit, NEG_INF, g)
        kn = jnp.concatenate([kn_ref[...], jnp.zeros((8 - n_new, GROUP_W), F32)], axis=0)
        s_own = _mm_nt(kn, qbd) * scale
        r8 = lax.broadcasted_iota(jnp.int32, (8, LANES), 0)
        l8 = lax.broadcasted_iota(jnp.int32, (8, LANES), 1)
        dist = (l8 >> 2) - r8
        own_bias = jnp.zeros((8, LANES), F32)
        for h in range(N_HEADS):
            for d in range(n_new):
                own_bias = jnp.where(((l8 & 3) == h) & (dist == d), relb_ref[h, d], own_bias)
        own_ok = (dist >= 0) & (r8 < n_new)
        s_own = jnp.where(own_ok, s_own + own_bias, NEG_INF)
        m_all = jnp.maximum(jnp.max(jnp.where(sel, m_scr[...], NEG_INF), axis=0, keepdims=True),
                            jnp.max(s_own, axis=0, keepdims=True))
        wgt = jnp.where(sel, jnp.exp(m_scr[...] - m_all), 0.0)
        p_own = jnp.exp(s_own - m_all)
        den = jnp.sum(wgt * l_scr[...], axis=0, keepdims=True) + jnp.sum(p_own, axis=0, keepdims=True)
        packed = jnp.concatenate([wgt, p_own, den, jnp.zeros((LANES - n_blocks - 9, LANES), F32)], axis=0)
        pk = packed.T
        out = jnp.zeros((n_pairs, GROUP_W), F32)
        for j in range(n_blocks):
            out = out + pk[0:n_pairs, j:j + 1] * acc_scr[j * n_pairs:(j + 1) * n_pairs, :]
        vn = vn_ref[...]
        for t in range(n_new):
            out = out + pk[0:n_pairs, n_blocks + t:n_blocks + t + 1] * vn[t:t + 1, :]
        out = out / pk[0:n_pairs, n_blocks + 8:n_blocks + 9]
        ph = lax.broadcasted_iota(jnp.int32, (n_pairs, HEAD_DIM), 0) & 3
        o16 = jnp.zeros((n_pairs, HEAD_DIM), F32)
        for h in range(N_HEADS):
            o16 = jnp.where(ph == h, out[:, h * HEAD_DIM:(h + 1) * HEAD_DIM], o16)
        for t in range(n_new):
            for h in range(N_HEADS):
                p = t * N_HEADS + h
                o_ref[t:t + 1, h * HEAD_DIM:(h + 1) * HEAD_DIM] = o16[p:p + 1, :]


def _moba_decode(q, k_new, v_new, cache_k, cache_v, page_table, rel_bias, pages_per_step=16):
    s, n_new, _ = q.shape
    n_pages = page_table.shape[1]
    page = cache_k.shape[1] // N_HEADS
    assert MOBA_BLOCK == 2 * page and n_pages % pages_per_step == 0 and pages_per_step % 16 == 0
    n_blocks = n_pages * page // MOBA_BLOCK
    assert n_blocks >= MOBA_TOPK and n_new <= 8
    npg = pages_per_step
    past = n_pages * page
    r = np.arange(MOBA_BLOCK)[:, None]
    lane_tok = (np.arange(LANES) >> 2)[None, :]
    bkt = jnp.asarray(_bucket_np(past + lane_tok - ((n_blocks - 1) * MOBA_BLOCK + r)))
    n_pairs = n_new * N_HEADS

    def page_spec(p):
        return pl.BlockSpec((None, page * N_HEADS, HEAD_DIM), lambda i, j, pt: (pt[i, j * npg + p], 0, 0))

    new_spec = pl.BlockSpec((None, n_new, GROUP_W), lambda i, j, pt: (i, 0, 0))
    kern = functools.partial(_moba_decode_kernel, pages_per_step=npg, n_blocks=n_blocks, n_new=n_new)
    grid_spec = pltpu.PrefetchScalarGridSpec(
        num_scalar_prefetch=1,
        grid=(s, n_pages // npg),
        in_specs=[pl.BlockSpec(memory_space=pltpu.SMEM),
                  pl.BlockSpec((MOBA_BLOCK, LANES), lambda i, j, pt: (0, 0)),
                  new_spec, new_spec, new_spec]
                 + [page_spec(p) for p in range(npg)] + [page_spec(p) for p in range(npg)],
        out_specs=new_spec,
        scratch_shapes=[pltpu.VMEM((LANES, GROUP_W), F32),
                        pltpu.VMEM((n_blocks, LANES), F32), pltpu.VMEM((n_blocks, LANES), F32),
                        pltpu.VMEM((n_blocks * n_pairs, GROUP_W), F32),
                        pltpu.VMEM((n_blocks, GROUP_W), F32),
                        pltpu.VMEM((MOBA_BLOCK, LANES), F32)],
    )
    return pl.pallas_call(
        kern,
        grid_spec=grid_spec,
        out_shape=jax.ShapeDtypeStruct((s, n_new, GROUP_W), F32),
        compiler_params=pltpu.CompilerParams(dimension_semantics=("arbitrary", "arbitrary"),
                                             vmem_limit_bytes=VMEM_LIMIT),
        name="moba_decode",
    )(page_table, rel_bias, bkt, q, k_new, v_new, *([cache_k] * npg), *([cache_v] * npg))


def _out_kernel(x_ref, od_ref, oa_ref, g1_ref, sh2_ref, sc2_ref, g2_ref, n2_ref, nf_ref,
                wo_ref, w1_ref, w2_ref, y_ref, *, ff_chunk):
    x = x_ref[...]
    mix = (jnp.dot(od_ref[...].astype(BF16), wo_ref[0:GROUP_W, :], preferred_element_type=F32)
           + jnp.dot(oa_ref[...].astype(BF16), wo_ref[GROUP_W:2 * GROUP_W, :], preferred_element_type=F32))
    x1 = x + g1_ref[...] * mix
    r = lax.rsqrt(jnp.mean(x1 * x1, axis=-1, keepdims=True) + EPS)
    h = ((x1 * r * n2_ref[...]) * (1.0 + sc2_ref[...]) + sh2_ref[...]).astype(BF16)
    d_ff = w1_ref.shape[1]
    ff = jnp.zeros(x.shape, F32)
    for c in range(d_ff // ff_chunk):
        a = jnp.dot(h, w1_ref[:, c * ff_chunk:(c + 1) * ff_chunk], preferred_element_type=F32)
        a = jnp.square(jnp.maximum(a, 0.0)).astype(BF16)
        ff = ff + jnp.dot(a, w2_ref[c * ff_chunk:(c + 1) * ff_chunk, :], preferred_element_type=F32)
    x2 = x1 + g2_ref[...] * ff
    r2 = lax.rsqrt(jnp.mean(x2 * x2, axis=-1, keepdims=True) + EPS)
    y_ref[...] = x2 * r2 * nf_ref[...]


def _out_mlp(x, o_d, o_a, g1, sh2, sc2, g2, n2, nf, w_out, w1, w2, tm, rows_per_mod, ff_chunk=1024):
    t, d = x.shape
    nt = t // tm
    if g1.shape[1] == 1:
        per = rows_per_mod // tm
        mod_spec = pl.BlockSpec((None, 1, d), lambda i: (i // per, 0, 0))
    else:
        mod_spec = pl.BlockSpec((None, tm, d), lambda i: (i, 0, 0))
    const = lambda i: (0, 0)
    row = lambda w: pl.BlockSpec((tm, w), lambda i: (i, 0))
    kern = functools.partial(_out_kernel, ff_chunk=ff_chunk)
    return pl.pallas_call(
        kern,
        grid=(nt,),
        in_specs=[row(d), row(GROUP_W), row(GROUP_W), mod_spec, mod_spec, mod_spec, mod_spec,
                  pl.BlockSpec((1, d), const), pl.BlockSpec((1, d), const),
                  pl.BlockSpec(w_out.shape, const), pl.BlockSpec(w1.shape, const),
                  pl.BlockSpec(w2.shape, const)],
        out_specs=row(d),
        out_shape=jax.ShapeDtypeStruct((t, d), F32),
        compiler_params=pltpu.CompilerParams(dimension_semantics=("arbitrary",),
                                             vmem_limit_bytes=VMEM_LIMIT),
        name="out_mlp",
    )(x, o_d, o_a, g1, sh2, sc2, g2, n2, nf, w_out, w1, w2)


def _split_w_in(w_in):
    ab0 = CONV_CH + GROUP_W
    w_main = jnp.concatenate([w_in[:, :ab0], w_in[:, ab0 + 2 * N_HEADS:]], axis=1).astype(BF16)
    w_ab = jnp.pad(w_in[:, ab0:ab0 + 2 * N_HEADS], ((0, 0), (0, LANES - 2 * N_HEADS)))
    w_ab_hi = w_ab.astype(BF16)
    w_ab_lo = (w_ab - w_ab_hi.astype(F32)).astype(BF16)
    return w_main, jnp.concatenate([w_ab_hi, w_ab_lo], axis=1)


def _lane_pad(v):
    return jnp.pad(v.astype(F32), (0, LANES - v.shape[0])).reshape(1, LANES)


def kernel(x_prompt, x_sample, cache_k, cache_v, state_delta, state_conv, page_table, c_prompt, c_sample,
           norm1_w, norm2_w, normf_w, w_ada, b_ada, w_in, w_out, conv_w, a_log, dt_bias, onorm_w,
           rel_bias, w_mlp1, w_mlp2):
    bp, tp, d = x_prompt.shape
    bs, ts, _ = x_sample.shape
    depth = w_in.shape[0]
    assert depth == 1 and d == 2 * GROUP_W
    l = 0
    n_pool, page = cache_k.shape[1], cache_k.shape[2]

    mod = _modulation(jnp.concatenate([c_prompt, c_sample], axis=0), w_ada[l], b_ada[l])
    mod = mod.reshape(bp + bs, 6, d)
    mod_p = [mod[:bp, i].reshape(bp, 1, d) for i in range(6)]
    tm_s = bs * ts
    mod_s = [jnp.broadcast_to(mod[bp:, i][:, None, :], (bs, ts, d)).reshape(1, tm_s, d) for i in range(6)]

    w_main, w_ab = _split_w_in(w_in[l])
    alog, dtb = _lane_pad(a_log[l]), _lane_pad(dt_bias[l])
    n1 = norm1_w[l].reshape(1, d)
    n2 = norm2_w[l].reshape(1, d)
    nf = normf_w.reshape(1, d)
    wo = w_out[l].astype(BF16)
    w1 = w_mlp1[l].astype(BF16)
    w2 = w_mlp2[l].astype(BF16)
    cw = conv_w[l]
    onw = onorm_w[l].reshape(1, HEAD_DIM)

    tm = 512
    xp = x_prompt.reshape(bp * tp, d)
    qkv_p, z_p, qa_p, ka_p, va_p, gb_p, gbt_p, ka4_p, va4_p = _inproj(
        xp, mod_p[0], mod_p[1], n1, w_main, w_ab, alog, dtb, tm, tp)
    n_chunks = 8
    od_p, s_p = _gdn(qkv_p.reshape(bp, tp, CONV_CH), gb_p.reshape(bp, tp, LANES),
                     gbt_p.reshape(8, bp, tp).transpose(1, 0, 2), z_p.reshape(bp, tp, GROUP_W),
                     jnp.zeros((bp, CONV_W - 1, CONV_CH), F32),
                     jnp.zeros((bp, N_HEADS, HEAD_DIM, HEAD_DIM), F32), cw, onw, n_chunks, 1, CHUNK)
    oa_p = _moba_prompt(qa_p.reshape(bp, tp, GROUP_W), ka_p.reshape(bp, tp, GROUP_W),
                        va_p.reshape(bp, tp, GROUP_W), rel_bias)
    y_p = _out_mlp(xp, od_p.reshape(bp * tp, GROUP_W), oa_p.reshape(bp * tp, GROUP_W),
                   mod_p[2], mod_p[3], mod_p[4], mod_p[5], n2, nf, wo, w1, w2, tm, tp)

    xs = x_sample.reshape(tm_s, d)
    qkv_s, z_s, qa_s, ka_s, va_s, gb_s, gbt_s, ka4_s, va4_s = _inproj(
        xs, mod_s[0], mod_s[1], n1, w_main, w_ab, alog, dtb, tm_s, tm_s)
    chunk_s = max(MIN_CHUNK, pl.next_power_of_2(ts))
    assert ts <= CHUNK
    gb_s3 = jnp.pad(gb_s.reshape(bs, ts, LANES), ((0, 0), (0, chunk_s - ts), (0, 0)))
    gbt_s3 = jnp.pad(gbt_s.reshape(8, bs, ts).transpose(1, 0, 2), ((0, 0), (0, 0), (0, chunk_s - ts)))
    od_s, s_s = _gdn(qkv_s.reshape(bs, ts, CONV_CH), gb_s3, gbt_s3, z_s.reshape(bs, ts, GROUP_W),
                     state_conv[l], state_delta[l], cw, onw, 1, next(n for n in (8, 4, 2, 1) if bs % n == 0),
                     chunk_s)
    oa_s = _moba_decode(qa_s.reshape(bs, ts, GROUP_W), ka_s.reshape(bs, ts, GROUP_W),
                        va_s.reshape(bs, ts, GROUP_W),
                        cache_k.reshape(depth * n_pool, page * N_HEADS, HEAD_DIM),
                        cache_v.reshape(depth * n_pool, page * N_HEADS, HEAD_DIM),
                        page_table, rel_bias)
    y_s = _out_mlp(xs, od_s.reshape(tm_s, GROUP_W), oa_s.reshape(tm_s, GROUP_W),
                   mod_s[2], mod_s[3], mod_s[4], mod_s[5], n2, nf, wo, w1, w2, tm_s, tm_s)

    shp_p = (1, bp, tp, N_HEADS, HEAD_DIM)
    shp_s = (1, bs, ts, N_HEADS, HEAD_DIM)
    qkv_p3 = qkv_p.reshape(bp, tp, CONV_CH)
    qkv_s3 = qkv_s.reshape(bs, ts, CONV_CH)
    conv_s = jnp.concatenate([state_conv[l], qkv_s3], axis=1)[:, ts:]
    return (y_p.reshape(bp, tp, d), y_s.reshape(bs, ts, d),
            ka4_p.reshape(shp_p), va4_p.reshape(shp_p), s_p[None], qkv_p3[:, tp - (CONV_W - 1):][None],
            ka4_s.reshape(shp_s), va4_s.reshape(shp_s), s_s[None], conv_s[None])
```
